```python
import jax, jax.numpy as jnp
from jax import lax
import numpy as np

D_MODEL = 4096
BATCH = 8
SEQ = 2048
DEPTH = 1
DEC_BATCH = 4
DEC_SEQ = 2048
PAST_LEN = 128

HEAD_DIM = 128
ATTN_WIDTH = D_MODEL // 2
N_HEADS = ATTN_WIDTH // HEAD_DIM
CONV_WIDTH = D_MODEL - ATTN_WIDTH
CONV_TAPS = 31
DILATED_BRANCHES = ((128, 1), (512, 4), (2048, 16))
QBLOCK = 64
ROPE_THETA = 10000.0
D_FF = -(-8 * D_MODEL // (3 * 256)) * 256
IN_COLS = 3 * ATTN_WIDTH + 2 * CONV_WIDTH
EPS = 1e-6
NEG = -1e30

kernel_name = "hybrid_dilated_attn_conformer_conv_encoder"


def _rmsnorm(x, g):
    xf = x.astype(jnp.float32)
    y = xf * lax.rsqrt(jnp.mean(xf * xf, axis=-1, keepdims=True) + EPS)
    return (y * g.astype(jnp.float32)).astype(x.dtype)


def _layernorm(x, g, b):
    xf = x.astype(jnp.float32)
    mu = jnp.mean(xf, axis=-1, keepdims=True)
    xc = xf - mu
    var = jnp.mean(xc * xc, axis=-1, keepdims=True)
    y = xc * lax.rsqrt(var + EPS) * g.astype(jnp.float32) + b.astype(jnp.float32)
    return y.astype(x.dtype)


def _rope(x):
    S = x.shape[1]
    half = HEAD_DIM // 2
    freqs = ROPE_THETA ** (-jnp.arange(half, dtype=jnp.float32) * 2.0 / HEAD_DIM)
    ang = jnp.arange(S, dtype=jnp.float32)[:, None] * freqs[None, :]
    cos = jnp.cos(ang)[None, :, None, :]
    sin = jnp.sin(ang)[None, :, None, :]
    xf = x.astype(jnp.float32)
    x1, x2 = xf[..., :half], xf[..., half:]
    out = jnp.concatenate([x1 * cos - x2 * sin, x1 * sin + x2 * cos], axis=-1)
    return out.astype(x.dtype)


def _dilated_branch(q, k, v, window, dilation):
    B, S, H, Dh = q.shape
    d = dilation
    R = window // (2 * d)
    L = S // d
    nblk = -(-L // QBLOCK)
    Lp = nblk * QBLOCK
    KB = QBLOCK + 2 * R

    def strided(t):
        return t.reshape(B, L, d, H, Dh).transpose(0, 2, 1, 3, 4)

    qs, ks, vs = strided(q), strided(k), strided(v)
    qb = jnp.pad(qs, ((0, 0), (0, 0), (0, Lp - L), (0, 0), (0, 0)))
    qb = qb.reshape(B, d, nblk, QBLOCK, H, Dh)
    kpad = ((0, 0), (0, 0), (R, R + Lp - L), (0, 0), (0, 0))
    idx = jnp.arange(nblk)[:, None] * QBLOCK + jnp.arange(KB)[None, :]
    kb = jnp.pad(ks, kpad)[:, :, idx]
    vb = jnp.pad(vs, kpad)[:, :, idx]

    scale = HEAD_DIM ** -0.5
    s = jnp.einsum('bgnqhd,bgnkhd->bgnhqk', qb, kb,
                   preferred_element_type=jnp.float32) * scale
    qpos = jnp.arange(nblk)[:, None, None] * QBLOCK + jnp.arange(QBLOCK)[None, :, None]
    kpos = jnp.arange(nblk)[:, None, None] * QBLOCK + jnp.arange(KB)[None, None, :] - R
    valid = (jnp.abs(kpos - qpos) <= R) & (kpos >= 0) & (kpos < L)
    s = jnp.where(valid[None, None, :, None, :, :], s, NEG)
    m = jnp.max(s, axis=-1, keepdims=True)
    p = jnp.exp(s - m)
    l = jnp.sum(p, axis=-1)
    o = jnp.einsum('bgnhqk,bgnkhd->bgnhqd', p.astype(v.dtype), vb,
                   preferred_element_type=jnp.float32) / l[..., None]
    lse = m[..., 0] + jnp.log(l)

    o = o.transpose(0, 1, 2, 4, 3, 5).reshape(B, d, Lp, H, Dh)[:, :, :L]
    o = o.transpose(0, 2, 1, 3, 4).reshape(B, S, H, Dh)
    lse = lse.transpose(0, 1, 2, 4, 3).reshape(B, d, Lp, H)[:, :, :L]
    lse = lse.transpose(0, 2, 1, 3).reshape(B, S, H)
    return o, lse


def _mixture_dilated_attention(q, k, v):
    outs, lses = [], []
    for window, dilation in DILATED_BRANCHES:
        o, lse = _dilated_branch(q, k, v, window, dilation)
        outs.append(o)
        lses.append(lse)
    w = jax.nn.softmax(jnp.stack(lses, axis=0), axis=0)
    out = jnp.einsum('rbsh,rbshd->bshd', w, jnp.stack(outs, axis=0))
    return out.astype(q.dtype)


def _conformer_conv(cv, cg, conv_w, conv_b, ln_g, ln_b):
    u = cv * jax.nn.sigmoid(cg)
    C = u.shape[-1]
    pad = (CONV_TAPS - 1) // 2
    y = lax.conv_general_dilated(
        u, conv_w.astype(u.dtype)[:, None, :], window_strides=(1,),
        padding=[(pad, pad)], dimension_numbers=('NWC', 'WIO', 'NWC'),
        feature_group_count=C)
    y = y + conv_b.astype(y.dtype)
    return jax.nn.silu(_layernorm(y, ln_g, ln_b))


def _layer(x, norm_mix_g, w_in, q_norm_g, k_norm_g, conv_w, conv_b, conv_ln_g,
           conv_ln_b, w_out, norm_ffn_g, w_gate, w_up, w_down):
    B, S, _ = x.shape
    hn = _rmsnorm(x, norm_mix_g)
    proj = hn @ w_in
    A, C = ATTN_WIDTH, CONV_WIDTH
    q, k, v, cv, cg = jnp.split(proj, [A, 2 * A, 3 * A, 3 * A + C], axis=-1)
    q = _rope(_rmsnorm(q.reshape(B, S, N_HEADS, HEAD_DIM), q_norm_g))
    k = _rope(_rmsnorm(k.reshape(B, S, N_HEADS, HEAD_DIM), k_norm_g))
    v = v.reshape(B, S, N_HEADS, HEAD_DIM)
    attn = _mixture_dilated_attention(q, k, v).reshape(B, S, A)
    conv = _conformer_conv(cv, cg, conv_w, conv_b, conv_ln_g, conv_ln_b)
    h = x + jnp.concatenate([attn, conv], axis=-1) @ w_out
    hf = _rmsnorm(h, norm_ffn_g)
    ffn = (jax.nn.silu(hf @ w_gate) * (hf @ w_up)) @ w_down
    return h + ffn


def setup_inputs(seed: int = 0) -> dict:
    key = jax.random.key(seed)
    ks = jax.random.split(key, 16)
    f32 = jnp.float32
    nrm = lambda k, shape, s: jax.random.normal(k, shape, f32) * s
    return {
        "x_prompt": nrm(ks[0], (BATCH, SEQ, D_MODEL), 1.0),
        "x_sample": nrm(ks[1], (DEC_BATCH, DEC_SEQ, D_MODEL), 1.0),
        "norm_mix_g": 1.0 + nrm(ks[2], (DEPTH, D_MODEL), 0.02),
        "w_in": nrm(ks[3], (DEPTH, D_MODEL, IN_COLS), D_MODEL ** -0.5),
        "q_norm_g": 1.0 + nrm(ks[4], (DEPTH, HEAD_DIM), 0.02),
        "k_norm_g": 1.0 + nrm(ks[5], (DEPTH, HEAD_DIM), 0.02),
        "conv_w": nrm(ks[6], (DEPTH, CONV_TAPS, CONV_WIDTH), CONV_TAPS ** -0.5),
        "conv_b": nrm(ks[7], (DEPTH, CONV_WIDTH), 0.02),
        "conv_ln_g": 1.0 + nrm(ks[8], (DEPTH, CONV_WIDTH), 0.02),
        "conv_ln_b": nrm(ks[9], (DEPTH, CONV_WIDTH), 0.02),
        "w_out": nrm(ks[10], (DEPTH, D_MODEL, D_MODEL), D_MODEL ** -0.5),
        "norm_ffn_g": 1.0 + nrm(ks[11], (DEPTH, D_MODEL), 0.02),
        "w_gate": nrm(ks[12], (DEPTH, D_MODEL, D_FF), D_MODEL ** -0.5),
        "w_up": nrm(ks[13], (DEPTH, D_MODEL, D_FF), D_MODEL ** -0.5),
        "w_down": nrm(ks[14], (DEPTH, D_FF, D_MODEL), D_FF ** -0.5),
    }


def reference(x_prompt, x_sample, norm_mix_g, w_in, q_norm_g, k_norm_g, conv_w,
              conv_b, conv_ln_g, conv_ln_b, w_out, norm_ffn_g, w_gate, w_up, w_down):
    y_prompt = x_prompt
    y_sample = x_sample
    for l in range(DEPTH):
        p = (norm_mix_g[l], w_in[l], q_norm_g[l], k_norm_g[l], conv_w[l], conv_b[l],
             conv_ln_g[l], conv_ln_b[l], w_out[l], norm_ffn_g[l], w_gate[l],
             w_up[l], w_down[l])
        y_prompt = _layer(y_prompt, *p)
        y_sample = _layer(y_sample, *p)
    return (y_prompt, y_sample)
```

```python
import functools

import numpy as np
import jax
import jax.numpy as jnp
from jax import lax
from jax.experimental import pallas as pl
from jax.experimental.pallas import tpu as pltpu

F32 = jnp.float32
BF16 = jnp.bfloat16

HEAD_DIM = 128
LANES = 128
CONV_TAPS = 31
CONV_PAD = (CONV_TAPS - 1) // 2
CONV_HALO = 16
DILATIONS = (1, 4, 16)
BAND = 64
QBLK = 128
KWIN = QBLK + 2 * BAND
ROPE_THETA = 10000.0
EPS = 1e-6
NEG = -1e30
VMEM_LIMIT_BYTES = 56 * 1024 * 1024


def _params(semantics):
    return pltpu.CompilerParams(dimension_semantics=semantics, vmem_limit_bytes=VMEM_LIMIT_BYTES)


def _inproj_kernel(x_ref, g_ref, wa_ref, wb_ref, cos_ref, sin_ref, qg_ref, kg_ref, o_ref, hn_ref,
                   *, n_q, n_qkv, scale, norm_rows):
    j = pl.program_id(1)
    tm, tn = o_ref.shape

    @pl.when(j == 0)
    def _():
        def body(t, c):
            rows = pl.ds(pl.multiple_of(t * norm_rows, norm_rows), norm_rows)
            x = x_ref[rows, :]
            ms = jnp.mean(x * x, axis=-1, keepdims=True)
            hn_ref[rows, :] = (x * lax.rsqrt(ms + EPS) * g_ref[...]).astype(BF16)
            return c
        lax.fori_loop(0, tm // norm_rows, body, 0)

    acc = jnp.dot(hn_ref[...], wa_ref[...], preferred_element_type=F32)

    @pl.when(j < 2 * n_q)
    def _():
        gain = jnp.where(j < n_q, qg_ref[...] * scale, kg_ref[...])
        cos = cos_ref[...]
        sin = sin_ref[...]
        for h in range(tn // HEAD_DIM):
            cols = slice(h * HEAD_DIM, (h + 1) * HEAD_DIM)
            xh = acc[:, cols]
            ms = jnp.mean(xh * xh, axis=-1, keepdims=True)
            y = xh * lax.rsqrt(ms + EPS) * gain
            y = y * cos + pltpu.roll(y, HEAD_DIM // 2, 1) * sin
            o_ref[:, cols] = y.astype(BF16)

    @pl.when((j >= 2 * n_q) & (j < n_qkv))
    def _():
        o_ref[...] = acc.astype(BF16)

    @pl.when(j >= n_qkv)
    def _():
        gate = jnp.dot(hn_ref[...], wb_ref[...], preferred_element_type=F32)
        o_ref[...] = (acc * jax.nn.sigmoid(gate)).astype(BF16)


def _in_projection(x2, norm_g, w_in_bf, cos_t, sin_t, qg, kg, *, seq, attn_w, conv_w, tm=512, tn=512):
    T, D = x2.shape
    n_q = attn_w // tn
    n_qkv = 3 * n_q
    n_c = conv_w // tn
    n_steps = n_qkv + n_c
    kern = functools.partial(_inproj_kernel, n_q=n_q, n_qkv=n_qkv, scale=HEAD_DIM ** -0.5, norm_rows=16)
    return pl.pallas_call(
        kern,
        grid=(T // tm, n_steps),
        in_specs=[
            pl.BlockSpec((tm, D), lambda i, j: (i, 0)),
            pl.BlockSpec((1, D), lambda i, j: (0, 0)),
            pl.BlockSpec((D, tn), lambda i, j: (0, j)),
            pl.BlockSpec((D, tn), lambda i, j: (0, jnp.where(j < n_qkv, n_steps, j + n_c))),
            pl.BlockSpec((tm, HEAD_DIM), lambda i, j: (i % (seq // tm), 0)),
            pl.BlockSpec((tm, HEAD_DIM), lambda i, j: (i % (seq // tm), 0)),
            pl.BlockSpec((1, HEAD_DIM), lambda i, j: (0, 0)),
            pl.BlockSpec((1, HEAD_DIM), lambda i, j: (0, 0)),
        ],
        out_specs=pl.BlockSpec((tm, tn), lambda i, j: (i, j)),
        out_shape=jax.ShapeDtypeStruct((T, 3 * attn_w + conv_w), BF16),
        scratch_shapes=[pltpu.VMEM((tm, D), BF16)],
        compiler_params=_params(("parallel", "arbitrary")),
        name="in_projection",
    )(x2, norm_g, w_in_bf, w_in_bf, cos_t, sin_t, qg, kg)


def _softmax_block(q, k, v, bias):
    s = lax.dot_general(q, k, (((1,), (1,)), ((), ())), preferred_element_type=F32) + bias
    m = jnp.max(s, axis=-1, keepdims=True)
    p = jnp.exp(s - m)
    l = jnp.sum(p, axis=-1, keepdims=True)
    o = jnp.dot(p.astype(BF16), v, preferred_element_type=F32) * (1.0 / l)
    return o, m + jnp.log(l)


def _attn_kernel(q_ref, k_ref, v_ref, bias_ref, bias16_ref, o_ref,
                 f32buf, qp4, kp4, vp4, qp16, kp16, vp16, ob1, ob4, ob16, ls1, ls4, ls16,
                 *, seq, heads):
    l4 = seq // 4
    l16 = seq // 16

    def put(ob, ls, rows, o, lse):
        ob[rows, :] = o
        ls[rows, :] = jnp.broadcast_to(lse, o.shape)

    for h in range(heads):
        cols = slice(h * HEAD_DIM, (h + 1) * HEAD_DIM)

        for src, p4, p16 in ((q_ref, qp4, qp16), (k_ref, kp4, kp16), (v_ref, vp4, vp16)):
            f32buf[...] = src[0, :, cols].astype(F32)
            for r in range(4):
                p4[r * l4:(r + 1) * l4, :] = f32buf[pl.ds(r, l4, stride=4), :].astype(BF16)
            for r in range(16):
                p16[r * l16:(r + 1) * l16, :] = f32buf[pl.ds(r, l16, stride=16), :].astype(BF16)

        def blk1(qs, ks, bias):
            o, lse = _softmax_block(q_ref[0, pl.ds(qs, QBLK), cols], k_ref[0, pl.ds(ks, KWIN), cols],
                                    v_ref[0, pl.ds(ks, KWIN), cols], bias)
            put(ob1, ls1, pl.ds(qs, QBLK), o, lse)

        blk1(0, 0, bias_ref[0])

        def body1(b, c):
            qs = pl.multiple_of(b * QBLK, QBLK)
            blk1(qs, pl.multiple_of(qs - BAND, BAND), bias_ref[1])
            return c
        lax.fori_loop(1, seq // QBLK - 1, body1, 0)
        blk1(seq - QBLK, seq - KWIN, bias_ref[2])

        def body4(r, c):
            base = pl.multiple_of(r * l4, l4)
            nb = l4 // QBLK
            for a in range(nb):
                ks = min(max(a * QBLK - BAND, 0), l4 - KWIN)
                bias = bias_ref[0 if a == 0 else (2 if a == nb - 1 else 1)]
                o, lse = _softmax_block(qp4[pl.ds(base + a * QBLK, QBLK), :], kp4[pl.ds(base + ks, KWIN), :],
                                        vp4[pl.ds(base + ks, KWIN), :], bias)
                put(ob4, ls4, pl.ds(4 * a * QBLK + r, QBLK, stride=4), o, lse)
            return c
        lax.fori_loop(0, 4, body4, 0)

        def body16(r, c):
            rows = pl.ds(pl.multiple_of(r * l16, l16), l16)
            o, lse = _softmax_block(qp16[rows, :], kp16[rows, :], vp16[rows, :], bias16_ref[...])
            put(ob16, ls16, pl.ds(r, l16, stride=16), o, lse)
            return c
        lax.fori_loop(0, 16, body16, 0)

        rc = 64

        def comb(t, c):
            rows = pl.ds(pl.multiple_of(t * rc, rc), rc)
            a1, a4, a16 = ls1[rows, :], ls4[rows, :], ls16[rows, :]
            mx = jnp.maximum(jnp.maximum(a1, a4), a16)
            e1, e4, e16 = jnp.exp(a1 - mx), jnp.exp(a4 - mx), jnp.exp(a16 - mx)
            out = (e1 * ob1[rows, :] + e4 * ob4[rows, :] + e16 * ob16[rows, :]) / (e1 + e4 + e16)
            o_ref[0, rows, cols] = out.astype(BF16)
            return c
        lax.fori_loop(0, seq // rc, comb, 0)


def _band_biases(seq):
    r = np.arange(QBLK)[:, None]
    c = np.arange(KWIN)[None, :]
    first = np.abs(c - r) <= BAND
    mid = np.abs(c - BAND - r) <= BAND
    last = np.abs(c - (KWIN - QBLK) - r) <= BAND
    b3 = np.where(np.stack([first, mid, last]), 0.0, NEG).astype(np.float32)
    c16 = np.arange(seq // 16)[None, :]
    r16 = np.arange(seq // 16)[:, None]
    b16 = np.where(np.abs(c16 - r16) <= BAND, 0.0, NEG).astype(np.float32)
    return jnp.asarray(b3), jnp.asarray(b16)


def _attention(proj3, *, attn_w, heads=2):
    B, S, _ = proj3.shape
    assert S // 16 == QBLK and S // 4 >= KWIN
    n_hg = attn_w // (heads * HEAD_DIM)
    hw = heads * HEAD_DIM
    b3, b16 = _band_biases(S)
    kern = functools.partial(_attn_kernel, seq=S, heads=heads)
    perm = lambda: pltpu.VMEM((S, HEAD_DIM), BF16)
    nat = lambda: pltpu.VMEM((S, HEAD_DIM), F32)
    return pl.pallas_call(
        kern,
        grid=(B, n_hg),
        in_specs=[
            pl.BlockSpec((1, S, hw), lambda b, g: (b, 0, g)),
            pl.BlockSpec((1, S, hw), lambda b, g: (b, 0, n_hg + g)),
            pl.BlockSpec((1, S, hw), lambda b, g: (b, 0, 2 * n_hg + g)),
            pl.BlockSpec((3, QBLK, KWIN), lambda b, g: (0, 0, 0)),
            pl.BlockSpec((S // 16, S // 16), lambda b, g: (0, 0)),
        ],
        out_specs=pl.BlockSpec((1, S, hw), lambda b, g: (b, 0, g)),
        out_shape=jax.ShapeDtypeStruct((B, S, attn_w), BF16),
        scratch_shapes=[nat()] + [perm() for _ in range(6)] + [nat() for _ in range(6)],
        compiler_params=_params(("parallel", "parallel")),
        name="dilated_attention",
    )(proj3, proj3, proj3, b3, b16)


def _conv_kernel(prev_ref, main_ref, next_ref, w_ref, b_ref, lg_ref, lb_ref, o_ref, ubuf, ybuf,
                 *, tiles_per_seq, row_blk):
    tt, C = main_ref.shape
    n_cb, _, col_blk = ybuf.shape
    pos = pl.program_id(0) % tiles_per_seq
    zero_halo = jnp.zeros((CONV_HALO, col_blk), F32)
    for cb in range(n_cb):
        cols = slice(cb * col_blk, (cb + 1) * col_blk)
        ubuf[cb, 0:CONV_HALO, :] = jnp.where(pos != 0, prev_ref[:, cols].astype(F32), zero_halo)
        ubuf[cb, CONV_HALO:CONV_HALO + tt, :] = main_ref[:, cols].astype(F32)
        ubuf[cb, CONV_HALO + tt:, :] = jnp.where(pos != tiles_per_seq - 1, next_ref[:, cols].astype(F32),
                                                 zero_halo)

    def conv_body(cb, c):
        for rb in range(tt // row_blk):
            acc = jnp.zeros((row_blk, col_blk), F32)
            for k in range(CONV_TAPS):
                r = rb * row_blk + CONV_HALO - CONV_PAD + k
                acc = acc + ubuf[cb, r:r + row_blk, :] * w_ref[cb, k:k + 1, :]
            ybuf[cb, rb * row_blk:(rb + 1) * row_blk, :] = acc + b_ref[cb]
        return c
    lax.fori_loop(0, n_cb, conv_body, 0)

    def norm_body(rb, c):
        rows = pl.ds(pl.multiple_of(rb * row_blk, row_blk), row_blk)
        ys = [ybuf[cb, rows, :] for cb in range(n_cb)]
        mu = sum(jnp.sum(y, axis=-1, keepdims=True) for y in ys) * (1.0 / C)
        ycs = [y - mu for y in ys]
        var = sum(jnp.sum(yc * yc, axis=-1, keepdims=True) for yc in ycs) * (1.0 / C)
        rstd = lax.rsqrt(var + EPS)
        for cb in range(n_cb):
            cols = slice(cb * col_blk, (cb + 1) * col_blk)
            z = ycs[cb] * rstd * lg_ref[:, cols] + lb_ref[:, cols]
            o_ref[rows, cols] = (z * jax.nn.sigmoid(z)).astype(BF16)
        return c
    lax.fori_loop(0, tt // row_blk, norm_body, 0)


def _conformer_conv(proj2, conv_w, conv_b, ln_g, ln_b, *, seq, conv_width, tt=256, col_blk=512):
    T, W = proj2.shape
    C = conv_width
    n_cb = C // col_blk
    ucol = (W - C) // C
    hb = tt // CONV_HALO
    n_halo = T // CONV_HALO
    w_blk = conv_w.reshape(CONV_TAPS, n_cb, col_blk).transpose(1, 0, 2)
    b_blk = conv_b.reshape(n_cb, 1, col_blk)
    kern = functools.partial(_conv_kernel, tiles_per_seq=seq // tt, row_blk=32)
    vec = lambda: pl.BlockSpec((1, C), lambda i: (0, 0))
    return pl.pallas_call(
        kern,
        grid=(T // tt,),
        in_specs=[
            pl.BlockSpec((CONV_HALO, C), lambda i: (jnp.maximum(i * hb - 1, 0), ucol)),
            pl.BlockSpec((tt, C), lambda i: (i, ucol)),
            pl.BlockSpec((CONV_HALO, C), lambda i: (jnp.minimum((i + 1) * hb, n_halo - 1), ucol)),
            pl.BlockSpec((n_cb, CONV_TAPS, col_blk), lambda i: (0, 0, 0)),
            pl.BlockSpec((n_cb, 1, col_blk), lambda i: (0, 0, 0)),
            vec(), vec(),
        ],
        out_specs=pl.BlockSpec((tt, C), lambda i: (i, 0)),
        out_shape=jax.ShapeDtypeStruct((T, C), BF16),
        scratch_shapes=[pltpu.VMEM((n_cb, tt + 2 * CONV_HALO, col_blk), F32),
                        pltpu.VMEM((n_cb, tt, col_blk), F32)],
        compiler_params=_params(("parallel",)),
        name="conformer_conv",
    )(proj2, proj2, proj2, w_blk, b_blk, ln_g, ln_b)


def _outproj_kernel(attn_ref, conv_ref, w_ref, x_ref, g_ref, h_ref, hb_ref, ssq_ref):
    j = pl.program_id(1)
    A = attn_ref.shape[1]
    acc = jnp.dot(attn_ref[...], w_ref[0:A, :], preferred_element_type=F32)
    acc = acc + jnp.dot(conv_ref[...], w_ref[A:, :], preferred_element_type=F32)
    h = x_ref[...] + acc
    h_ref[...] = h
    hb_ref[...] = (h * g_ref[...]).astype(BF16)
    hh = h * h
    part = hh[:, 0:LANES]
    for c in range(1, h.shape[1] // LANES):
        part = part + hh[:, c * LANES:(c + 1) * LANES]

    @pl.when(j == 0)
    def _():
        ssq_ref[...] = part

    @pl.when(j != 0)
    def _():
        ssq_ref[...] += part


def _out_projection(attn2, conv2, w_out_bf, x2, ffn_g, *, tm=1024, tn=512):
    T, D = x2.shape
    A = attn2.shape[1]
    C = conv2.shape[1]
    return pl.pallas_call(
        _outproj_kernel,
        grid=(T // tm, D // tn),
        in_specs=[
            pl.BlockSpec((tm, A), lambda i, j: (i, 0)),
            pl.BlockSpec((tm, C), lambda i, j: (i, 0)),
            pl.BlockSpec((A + C, tn), lambda i, j: (0, j)),
            pl.BlockSpec((tm, tn), lambda i, j: (i, j)),
            pl.BlockSpec((1, tn), lambda i, j: (0, j)),
        ],
        out_specs=[
            pl.BlockSpec((tm, tn), lambda i, j: (i, j)),
            pl.BlockSpec((tm, tn), lambda i, j: (i, j)),
            pl.BlockSpec((tm, LANES), lambda i, j: (i, 0)),
        ],
        out_shape=[
            jax.ShapeDtypeStruct((T, D), F32),
            jax.ShapeDtypeStruct((T, D), BF16),
            jax.ShapeDtypeStruct((T, LANES), F32),
        ],
        compiler_params=_params(("parallel", "arbitrary")),
        name="out_projection",
    )(attn2, conv2, w_out_bf, x2, ffn_g)


def _ffn_kernel(hb_ref, ssq_ref, h_ref, wg_ref, wu_ref, wd_ref, o_ref, r_ref, *, d_model):
    j = pl.program_id(1)

    @pl.when(j == 0)
    def _():
        ms = jnp.sum(ssq_ref[...], axis=-1, keepdims=True) * (1.0 / d_model)
        r_ref[...] = jnp.broadcast_to(lax.rsqrt(ms + EPS), r_ref.shape)
        o_ref[...] = h_ref[...]

    hb = hb_ref[...]
    r = r_ref[...]
    g = jnp.dot(hb, wg_ref[...], preferred_element_type=F32) * r
    u = jnp.dot(hb, wu_ref[...], preferred_element_type=F32) * r
    a = (g * jax.nn.sigmoid(g) * u).astype(BF16)
    o_ref[...] += jnp.dot(a, wd_ref[...], preferred_element_type=F32)


def _ffn(hb, ssq, h, wg_bf, wu_bf, wd_bf, *, tm=512, tf=256):
    T, D = h.shape
    F = wg_bf.shape[1]
    kern = functools.partial(_ffn_kernel, d_model=D)
    once = dict(pipeline_mode=pl.Buffered(1))
    return pl.pallas_call(
        kern,
        grid=(T // tm, F // tf),
        in_specs=[
            pl.BlockSpec((tm, D), lambda i, j: (i, 0), **once),
            pl.BlockSpec((tm, LANES), lambda i, j: (i, 0)),
            pl.BlockSpec((tm, D), lambda i, j: (i, 0), **once),
            pl.BlockSpec((D, tf), lambda i, j: (0, j)),
            pl.BlockSpec((D, tf), lambda i, j: (0, j)),
            pl.BlockSpec((tf, D), lambda i, j: (j, 0)),
        ],
        out_specs=pl.BlockSpec((tm, D), lambda i, j: (i, 0)),
        out_shape=jax.ShapeDtypeStruct((T, D), F32),
        scratch_shapes=[pltpu.VMEM((tm, tf), F32)],
        compiler_params=_params(("parallel", "arbitrary")),
        name="swiglu_ffn",
    )(hb, ssq, h, wg_bf, wu_bf, wd_bf)


def _rope_tables(seq):
    half = HEAD_DIM // 2
    freqs = ROPE_THETA ** (-jnp.arange(half, dtype=F32) * 2.0 / HEAD_DIM)
    ang = jnp.arange(seq, dtype=F32)[:, None] * freqs[None, :]
    cos = jnp.cos(ang)
    sin = jnp.sin(ang)
    return jnp.concatenate([cos, cos], axis=-1), jnp.concatenate([-sin, sin], axis=-1)


def _layer(x, p):
    B, S, D = x.shape
    T = B * S
    attn_w = D // 2
    conv_w = D - attn_w
    x2 = x.reshape(T, D)
    proj = _in_projection(x2, p["norm_mix_g"], p["w_in"], p["cos"], p["sin"], p["q_norm_g"], p["k_norm_g"],
                          seq=S, attn_w=attn_w, conv_w=conv_w)
    attn = _attention(proj.reshape(B, S, -1), attn_w=attn_w)
    conv = _conformer_conv(proj, p["conv_w"], p["conv_b"], p["conv_ln_g"], p["conv_ln_b"],
                           seq=S, conv_width=conv_w)
    h, hb, ssq = _out_projection(attn.reshape(T, attn_w), conv, p["w_out"], x2, p["norm_ffn_g"])
    y = _ffn(hb, ssq, h, p["w_gate"], p["w_up"], p["w_down"])
    return y.reshape(B, S, D)


def kernel(x_prompt, x_sample, norm_mix_g, w_in, q_norm_g, k_norm_g, conv_w, conv_b, conv_ln_g, conv_ln_b,
           w_out, norm_ffn_g, w_gate, w_up, w_down):
    depth = w_in.shape[0]
    y_prompt, y_sample = x_prompt, x_sample
    cos, sin = _rope_tables(x_prompt.shape[1])
    assert x_sample.shape[1] == x_prompt.shape[1]
    row = lambda v: v.reshape(1, -1)
    for l in range(depth):
        p = dict(
            norm_mix_g=row(norm_mix_g[l]), w_in=w_in[l].astype(BF16), cos=cos, sin=sin,
            q_norm_g=row(q_norm_g[l]), k_norm_g=row(k_norm_g[l]),
            conv_w=conv_w[l], conv_b=row(conv_b[l]), conv_ln_g=row(conv_ln_g[l]), conv_ln_b=row(conv_ln_b[l]),
            w_out=w_out[l].astype(BF16), norm_ffn_g=row(norm_ffn_g[l]),
            w_gate=w_gate[l].astype(BF16), w_up=w_up[l].astype(BF16), w_down=w_down[l].astype(BF16),
        )
        y_prompt = _layer(y_prompt, p)
        y_sample = _layer(y_sample, p)
    return (y_prompt, y_sample)
```

```python
import functools

import numpy as np
import jax
import jax.numpy as jnp
from jax import lax
from jax.experimental import pallas as pl
from jax.experimental.pallas import tpu as pltpu

F32 = jnp.float32
BF16 = jnp.bfloat16

HEAD_DIM = 128
LANES = 128
CONV_TAPS = 31
CONV_PAD = (CONV_TAPS - 1) // 2
CONV_HALO = 16
DILATIONS = (1, 4, 16)
BAND = 64
QBLK = 128
KWIN = QBLK + 2 * BAND
GROUP = 16
ROPE_THETA = 10000.0
EPS = 1e-6
NEG = -1e30
VMEM_LIMIT_BYTES = 56 * 1024 * 1024


def _params(semantics):
    return pltpu.CompilerParams(dimension_semantics=semantics, vmem_limit_bytes=VMEM_LIMIT_BYTES)


def _inproj_kernel(x_ref, g_ref, wa_ref, wb_ref, cos_ref, sin_ref, qg_ref, kg_ref, o_ref, hn_ref,
                   *, n_q, n_qkv, scale, norm_rows):
    j = pl.program_id(1)
    tm, tn = o_ref.shape

    @pl.when(j == 0)
    def _():
        def body(t, c):
            rows = pl.ds(pl.multiple_of(t * norm_rows, norm_rows), norm_rows)
            x = x_ref[rows, :]
            ms = jnp.mean(x * x, axis=-1, keepdims=True)
            hn_ref[rows, :] = (x * lax.rsqrt(ms + EPS) * g_ref[...]).astype(BF16)
            return c
        lax.fori_loop(0, tm // norm_rows, body, 0, unroll=2)

    acc = jnp.dot(hn_ref[...], wa_ref[...], preferred_element_type=F32)

    @pl.when(j < 2 * n_q)
    def _():
        gain = jnp.where(j < n_q, qg_ref[...] * scale, kg_ref[...])
        cos = cos_ref[...]
        sin = sin_ref[...]
        heads = [acc[:, h * HEAD_DIM:(h + 1) * HEAD_DIM] for h in range(tn // HEAD_DIM)]
        ms = [jnp.mean(xh * xh, axis=-1, keepdims=True) for xh in heads]
        ys = [xh * lax.rsqrt(m + EPS) * gain for xh, m in zip(heads, ms)]
        rot = [pltpu.roll(y, HEAD_DIM // 2, 1) for y in ys]
        for h, (y, yr) in enumerate(zip(ys, rot)):
            o_ref[:, h * HEAD_DIM:(h + 1) * HEAD_DIM] = (y * cos + yr * sin).astype(BF16)

    @pl.when((j >= 2 * n_q) & (j < n_qkv))
    def _():
        o_ref[...] = acc.astype(BF16)

    @pl.when(j >= n_qkv)
    def _():
        gate = jnp.dot(hn_ref[...], wb_ref[...], preferred_element_type=F32)
        o_ref[...] = (acc * jax.nn.sigmoid(gate)).astype(BF16)


def _in_projection(x2, norm_g, w_in_bf, cos_t, sin_t, qg, kg, *, seq, attn_w, conv_w, tm=512, tn=512):
    T, D = x2.shape
    n_q = attn_w // tn
    n_qkv = 3 * n_q
    n_c = conv_w // tn
    n_steps = n_qkv + n_c
    kern = functools.partial(_inproj_kernel, n_q=n_q, n_qkv=n_qkv, scale=HEAD_DIM ** -0.5, norm_rows=16)
    return pl.pallas_call(
        kern,
        grid=(T // tm, n_steps),
        in_specs=[
            pl.BlockSpec((tm, D), lambda i, j: (i, 0)),
            pl.BlockSpec((1, D), lambda i, j: (0, 0)),
            pl.BlockSpec((D, tn), lambda i, j: (0, j)),
            pl.BlockSpec((D, tn), lambda i, j: (0, jnp.where(j < n_qkv, n_steps, j + n_c))),
            pl.BlockSpec((tm, HEAD_DIM), lambda i, j: (i % (seq // tm), 0)),
            pl.BlockSpec((tm, HEAD_DIM), lambda i, j: (i % (seq // tm), 0)),
            pl.BlockSpec((1, HEAD_DIM), lambda i, j: (0, 0)),
            pl.BlockSpec((1, HEAD_DIM), lambda i, j: (0, 0)),
        ],
        out_specs=pl.BlockSpec((tm, tn), lambda i, j: (i, j)),
        out_shape=jax.ShapeDtypeStruct((T, 3 * attn_w + conv_w), BF16),
        scratch_shapes=[pltpu.VMEM((tm, D), BF16)],
        compiler_params=_params(("parallel", "arbitrary")),
        name="in_projection",
    )(x2, norm_g, w_in_bf, w_in_bf, cos_t, sin_t, qg, kg)


def _softmax_blocks(q, k, v, bias):
    s = jnp.einsum("gqd,gkd->gqk", q, k, preferred_element_type=F32) + bias
    m = jnp.max(s, axis=-1, keepdims=True)
    p = jnp.exp(s - m)
    l = jnp.sum(p, axis=-1, keepdims=True)
    o = jnp.einsum("gqk,gkd->gqd", p.astype(BF16), v, preferred_element_type=F32) * (1.0 / l)
    return o, m + jnp.log(l)


def _attn_kernel(q_ref, k_ref, v_ref, bias1_ref, bias4_ref, bias16_ref, o_ref,
                 f32buf, f32buf4, qp4, kp4, vp4, qp16, kp16, vp16, ob1, ob4, ob16, ls1, ls4, ls16,
                 *, seq, heads):
    l4 = seq // 4
    l16 = seq // 16
    nb4 = l4 // QBLK
    ng1 = seq // (GROUP * QBLK)
    grows = GROUP * QBLK
    blocks = lambda x: x.reshape(GROUP, QBLK, HEAD_DIM)

    def put(ob, ls, rows, o, lse):
        ob[rows, :] = o
        ls[rows, :] = jnp.broadcast_to(lse, o.shape)

    for h in range(heads):
        cols = slice(h * HEAD_DIM, (h + 1) * HEAD_DIM)

        for src, p4, p16 in ((q_ref, qp4, qp16), (k_ref, kp4, kp16), (v_ref, vp4, vp16)):
            f32buf[...] = src[0, :, cols].astype(F32)
            for r4 in range(4):
                x4 = f32buf[pl.ds(r4, l4, stride=4), :]
                f32buf4[r4 * l4:(r4 + 1) * l4, :] = x4
                p4[r4 * l4:(r4 + 1) * l4, :] = x4.astype(BF16)
            for r in range(16):
                a, r4 = divmod(r, 4)
                p16[r * l16:(r + 1) * l16, :] = f32buf4[pl.ds(r4 * l4 + a, l16, stride=4), :].astype(BF16)


        def body1(g, c):
            rows = pl.ds(pl.multiple_of(g * grows, grows), grows)
            starts = [pl.multiple_of(jnp.clip((g * GROUP + j) * QBLK - BAND, 0, seq - KWIN), BAND)
                      for j in range(GROUP)]
            k3 = jnp.stack([k_ref[0, pl.ds(ks, KWIN), cols] for ks in starts])
            v3 = jnp.stack([v_ref[0, pl.ds(ks, KWIN), cols] for ks in starts])
            o, lse = _softmax_blocks(blocks(q_ref[0, rows, cols]), k3, v3, bias1_ref[g])
            put(ob1, ls1, rows, o.reshape(grows, HEAD_DIM), lse.reshape(grows, 1))
            return c
        lax.fori_loop(0, ng1, body1, 0)

        def body4(g, c):
            base = pl.multiple_of(g * grows, grows)
            rows = pl.ds(base, grows)
            starts = [(j // nb4) * l4 + min(max((j % nb4) * QBLK - BAND, 0), l4 - KWIN) for j in range(GROUP)]
            k3 = jnp.stack([kp4[pl.ds(base + ks, KWIN), :] for ks in starts])
            v3 = jnp.stack([vp4[pl.ds(base + ks, KWIN), :] for ks in starts])
            o, lse = _softmax_blocks(blocks(qp4[rows, :]), k3, v3, bias4_ref[...])
            put(ob4, ls4, rows, o.reshape(grows, HEAD_DIM), lse.reshape(grows, 1))
            return c
        lax.fori_loop(0, seq // grows, body4, 0)

        def body16(g, c):
            rows = pl.ds(pl.multiple_of(g * grows, grows), grows)
            o, lse = _softmax_blocks(blocks(qp16[rows, :]), blocks(kp16[rows, :]), blocks(vp16[rows, :]),
                                     bias16_ref[...])
            for j in range(GROUP):
                a = g * (GROUP // 4) + j // 4
                put(ob16, ls16, pl.ds((j % 4) * l4 + a, l16, stride=4), o[j], lse[j])
            return c
        lax.fori_loop(0, 16 // GROUP, body16, 0)

        rc = 64
        for r4 in range(4):
            def comb(t, c, r4=r4):
                rows = pl.ds(pl.multiple_of(r4 * l4 + t * rc, rc), rc)
                nat = pl.ds(4 * t * rc + r4, rc, stride=4)
                a1, a4, a16 = ls1[nat, :], ls4[rows, :], ls16[rows, :]
                mx = jnp.maximum(jnp.maximum(a1, a4), a16)
                e1, e4, e16 = jnp.exp(a1 - mx), jnp.exp(a4 - mx), jnp.exp(a16 - mx)
                out = (e1 * ob1[nat, :] + e4 * ob4[rows, :] + e16 * ob16[rows, :]) / (e1 + e4 + e16)
                f32buf[nat, :] = out
                return c
            lax.fori_loop(0, l4 // rc, comb, 0, unroll=2)
        o_ref[0, :, cols] = f32buf[...].astype(BF16)


def _band_biases(seq):
    r = np.arange(QBLK)[:, None]
    c = np.arange(KWIN)[None, :]
    first = np.abs(c - r) <= BAND
    mid = np.abs(c - BAND - r) <= BAND
    last = np.abs(c - (KWIN - QBLK) - r) <= BAND
    neg = lambda valid: np.where(valid, 0.0, NEG).astype(np.float32)
    nb1 = seq // QBLK
    b1 = neg(np.stack([first] + [mid] * (nb1 - 2) + [last])).reshape(nb1 // GROUP, GROUP, QBLK, KWIN)
    nb4 = seq // (4 * QBLK)
    b4 = neg(np.stack(([first] + [mid] * (nb4 - 2) + [last]) * (GROUP // nb4)))
    c16 = np.arange(seq // 16)[None, :]
    r16 = np.arange(seq // 16)[:, None]
    b16 = neg(np.abs(c16 - r16) <= BAND)
    return jnp.asarray(b1), jnp.asarray(b4), jnp.asarray(b16)


def _attention(proj3, *, attn_w, heads=2):
    B, S, _ = proj3.shape
    assert S // 16 == QBLK and GROUP % (S // (4 * QBLK)) == 0 and S % (GROUP * QBLK) == 0 and GROUP % 4 == 0
    n_hg = attn_w // (heads * HEAD_DIM)
    hw = heads * HEAD_DIM
    b1, b4, b16 = _band_biases(S)
    kern = functools.partial(_attn_kernel, seq=S, heads=heads)
    perm = lambda: pltpu.VMEM((S, HEAD_DIM), BF16)
    nat = lambda: pltpu.VMEM((S, HEAD_DIM), F32)
    return pl.pallas_call(
        kern,
        grid=(B, n_hg),
        in_specs=[
            pl.BlockSpec((1, S, hw), lambda b, g: (b, 0, g)),
            pl.BlockSpec((1, S, hw), lambda b, g: (b, 0, n_hg + g)),
            pl.BlockSpec((1, S, hw), lambda b, g: (b, 0, 2 * n_hg + g)),
            pl.BlockSpec((S // (GROUP * QBLK), GROUP, QBLK, KWIN), lambda b, g: (0, 0, 0, 0)),
            pl.BlockSpec((GROUP, QBLK, KWIN), lambda b, g: (0, 0, 0)),
            pl.BlockSpec((S // 16, S // 16), lambda b, g: (0, 0)),
        ],
        out_specs=pl.BlockSpec((1, S, hw), lambda b, g: (b, 0, g)),
        out_shape=jax.ShapeDtypeStruct((B, S, attn_w), BF16),
        scratch_shapes=[nat(), nat()] + [perm() for _ in range(6)] + [nat() for _ in range(6)],
        compiler_params=_params(("parallel", "parallel")),
        name="dilated_attention",
    )(proj3, proj3, proj3, b1, b4, b16)


def _conv_kernel(prev_ref, main_ref, next_ref, w_ref, b_ref, lg_ref, lb_ref, o_ref, ubuf, ybuf,
                 *, tiles_per_seq, row_blk, norm_rows):
    tt, C = main_ref.shape
    n_cb, _, col_blk = ybuf.shape
    pos = pl.program_id(0) % tiles_per_seq
    zero_halo = jnp.zeros((CONV_HALO, col_blk), F32)
    for cb in range(n_cb):
        cols = slice(cb * col_blk, (cb + 1) * col_blk)
        ubuf[cb, 0:CONV_HALO, :] = jnp.where(pos != 0, prev_ref[:, cols].astype(F32), zero_halo)
        ubuf[cb, CONV_HALO:CONV_HALO + tt, :] = main_ref[:, cols].astype(F32)
        ubuf[cb, CONV_HALO + tt:, :] = jnp.where(pos != tiles_per_seq - 1, next_ref[:, cols].astype(F32),
                                                 zero_halo)

    def conv_body(cb, c):
        for rb in range(tt // (2 * row_blk)):
            for parity in range(2):
                first = rb * 2 * row_blk + parity
                acc = jnp.zeros((row_blk, col_blk), F32)
                for k in range(CONV_TAPS):
                    r = first + CONV_HALO - CONV_PAD + k
                    acc = acc + ubuf[cb, pl.ds(r, row_blk, stride=2), :] * w_ref[cb, k:k + 1, :]
                ybuf[cb, pl.ds(first, row_blk, stride=2), :] = acc + b_ref[cb]
        return c
    lax.fori_loop(0, n_cb, conv_body, 0)

    def norm_body(rb, c):
        rows = pl.ds(pl.multiple_of(rb * norm_rows, norm_rows), norm_rows)
        ys = [ybuf[cb, rows, :] for cb in range(n_cb)]
        mu = jnp.sum(sum(ys), axis=-1, keepdims=True) * (1.0 / C)
        ycs = [y - mu for y in ys]
        var = jnp.sum(sum(yc * yc for yc in ycs), axis=-1, keepdims=True) * (1.0 / C)
        rstd = lax.rsqrt(var + EPS)
        for cb in range(n_cb):
            cols = slice(cb * col_blk, (cb + 1) * col_blk)
            z = ycs[cb] * rstd * lg_ref[:, cols] + lb_ref[:, cols]
            o_ref[rows, cols] = (z * jax.nn.sigmoid(z)).astype(BF16)
        return c
    lax.fori_loop(0, tt // norm_rows, norm_body, 0, unroll=4)


def _conformer_conv(proj2, conv_w, conv_b, ln_g, ln_b, *, seq, conv_width, tt=256, col_blk=LANES):
    T, W = proj2.shape
    C = conv_width
    n_cb = C // col_blk
    ucol = (W - C) // C
    hb = tt // CONV_HALO
    n_halo = T // CONV_HALO
    w_blk = conv_w.reshape(CONV_TAPS, n_cb, col_blk).transpose(1, 0, 2)
    b_blk = conv_b.reshape(n_cb, 1, col_blk)
    kern = functools.partial(_conv_kernel, tiles_per_seq=seq // tt, row_blk=tt // 2, norm_rows=8)
    vec = lambda: pl.BlockSpec((1, C), lambda i: (0, 0))
    return pl.pallas_call(
        kern,
        grid=(T // tt,),
        in_specs=[
            pl.BlockSpec((CONV_HALO, C), lambda i: (jnp.maximum(i * hb - 1, 0), ucol)),
            pl.BlockSpec((tt, C), lambda i: (i, ucol)),
            pl.BlockSpec((CONV_HALO, C), lambda i: (jnp.minimum((i + 1) * hb, n_halo - 1), ucol)),
            pl.BlockSpec((n_cb, CONV_TAPS, col_blk), lambda i: (0, 0, 0)),
            pl.BlockSpec((n_cb, 1, col_blk), lambda i: (0, 0, 0)),
            vec(), vec(),
        ],
        out_specs=pl.BlockSpec((tt, C), lambda i: (i, 0)),
        out_shape=jax.ShapeDtypeStruct((T, C), BF16),
        scratch_shapes=[pltpu.VMEM((n_cb, tt + 2 * CONV_HALO, col_blk), F32),
                        pltpu.VMEM((n_cb, tt, col_blk), F32)],
        compiler_params=_params(("parallel",)),
        name="conformer_conv",
    )(proj2, proj2, proj2, w_blk, b_blk, ln_g, ln_b)


def _outproj_kernel(attn_ref, conv_ref, w_ref, x_ref, g_ref, h_ref, hb_ref, ssq_ref):
    j = pl.program_id(1)
    A = attn_ref.shape[1]
    acc = jnp.dot(attn_ref[...], w_ref[0:A, :], preferred_element_type=F32)
    acc = acc + jnp.dot(conv_ref[...], w_ref[A:, :], preferred_element_type=F32)
    h = x_ref[...] + acc
    h_ref[...] = h
    hb_ref[...] = (h * g_ref[...]).astype(BF16)
    hh = h * h
    part = hh[:, 0:LANES]
    for c in range(1, h.shape[1] // LANES):
        part = part + hh[:, c * LANES:(c + 1) * LANES]

    @pl.when(j == 0)
    def _():
        ssq_ref[...] = part

    @pl.when(j != 0)
    def _():
        ssq_ref[...] += part


def _out_projection(attn2, conv2, w_out_bf, x2, ffn_g, *, tm=1024, tn=512):
    T, D = x2.shape
    A = attn2.shape[1]
    C = conv2.shape[1]
    return pl.pallas_call(
        _outproj_kernel,
        grid=(T // tm, D // tn),
        in_specs=[
            pl.BlockSpec((tm, A), lambda i, j: (i, 0)),
            pl.BlockSpec((tm, C), lambda i, j: (i, 0)),
            pl.BlockSpec((A + C, tn), lambda i, j: (0, j)),
            pl.BlockSpec((tm, tn), lambda i, j: (i, j)),
            pl.BlockSpec((1, tn), lambda i, j: (0, j)),
        ],
        out_specs=[
            pl.BlockSpec((tm, tn), lambda i, j: (i, j)),
            pl.BlockSpec((tm, tn), lambda i, j: (i, j)),
            pl.BlockSpec((tm, LANES), lambda i, j: (i, 0)),
        ],
        out_shape=[
            jax.ShapeDtypeStruct((T, D), F32),
            jax.ShapeDtypeStruct((T, D), BF16),
            jax.ShapeDtypeStruct((T, LANES), F32),
        ],
        compiler_params=_params(("parallel", "arbitrary")),
        name="out_projection",
    )(attn2, conv2, w_out_bf, x2, ffn_g)


def _ffn_kernel(hb_ref, ssq_ref, h_ref, wg_ref, wu_ref, wd_ref, o_ref, r_ref, *, d_model):
    j = pl.program_id(1)

    @pl.when(j == 0)
    def _():
        ms = jnp.sum(ssq_ref[...], axis=-1, keepdims=True) * (1.0 / d_model)
        r_ref[...] = jnp.broadcast_to(lax.rsqrt(ms + EPS), r_ref.shape)
        o_ref[...] = h_ref[...]

    hb = hb_ref[...]
    r = r_ref[...]
    g = jnp.dot(hb, wg_ref[...], preferred_element_type=F32) * r
    u = jnp.dot(hb, wu_ref[...], preferred_element_type=F32) * r
    a = (g * jax.nn.sigmoid(g) * u).astype(BF16)
    o_ref[...] += jnp.dot(a, wd_ref[...], preferred_element_type=F32)


def _ffn(hb, ssq, h, wg_bf, wu_bf, wd_bf, *, tm=512, tf=256):
    T, D = h.shape
    F = wg_bf.shape[1]
    kern = functools.partial(_ffn_kernel, d_model=D)
    once = dict(pipeline_mode=pl.Buffered(1))
    return pl.pallas_call(
        kern,
        grid=(T // tm, F // tf),
        in_specs=[
            pl.BlockSpec((tm, D), lambda i, j: (i, 0), **once),
            pl.BlockSpec((tm, LANES), lambda i, j: (i, 0)),
            pl.BlockSpec((tm, D), lambda i, j: (i, 0), **once),
            pl.BlockSpec((D, tf), lambda i, j: (0, j)),
            pl.BlockSpec((D, tf), lambda i, j: (0, j)),
            pl.BlockSpec((tf, D), lambda i, j: (j, 0)),
        ],
        out_specs=pl.BlockSpec((tm, D), lambda i, j: (i, 0)),
        out_shape=jax.ShapeDtypeStruct((T, D), F32),
        scratch_shapes=[pltpu.VMEM((tm, tf), F32)],
        compiler_params=_params(("parallel", "arbitrary")),
        name="swiglu_ffn",
    )(hb, ssq, h, wg_bf, wu_bf, wd_bf)


def _rope_tables(seq):
    half = HEAD_DIM // 2
    freqs = ROPE_THETA ** (-jnp.arange(half, dtype=F32) * 2.0 / HEAD_DIM)
    ang = jnp.arange(seq, dtype=F32)[:, None] * freqs[None, :]
    cos = jnp.cos(ang)
    sin = jnp.sin(ang)
    return jnp.concatenate([cos, cos], axis=-1), jnp.concatenate([-sin, sin], axis=-1)


def _layer(x, p):
    B, S, D = x.shape
    T = B * S
    attn_w = D // 2
    conv_w = D - attn_w
    x2 = x.reshape(T, D)
    proj = _in_projection(x2, p["norm_mix_g"], p["w_in"], p["cos"], p["sin"], p["q_norm_g"], p["k_norm_g"],
                          seq=S, attn_w=attn_w, conv_w=conv_w)
    attn = _attention(proj.reshape(B, S, -1), attn_w=attn_w)
    conv = _conformer_conv(proj, p["conv_w"], p["conv_b"], p["conv_ln_g"], p["conv_ln_b"],
                           seq=S, conv_width=conv_w)
    h, hb, ssq = _out_projection(attn.reshape(T, attn_w), conv, p["w_out"], x2, p["norm_ffn_g"])
    y = _ffn(hb, ssq, h, p["w_gate"], p["w_up"], p["w_down"])
    return y.reshape(B, S, D)


def kernel(x_prompt, x_sample, norm_mix_g, w_in, q_norm_g, k_norm_g, conv_w, conv_b, conv_ln_g, conv_ln_b,
           w_out, norm_ffn_g, w_gate, w_up, w_down):
    depth = w_in.shape[0]
    y_prompt, y_sample = x_prompt, x_sample
    cos, sin = _rope_tables(x_prompt.shape[1])
    assert x_sample.shape[1] == x_prompt.shape[1]
    row = lambda v: v.reshape(1, -1)
    for l in range(depth):
        p = dict(
            norm_mix_g=row(norm_mix_g[l]), w_in=w_in[l].astype(BF16), cos=cos, sin=sin,
            q_norm_g=row(q_norm_g[l]), k_norm_g=row(k_norm_g[l]),
            conv_w=conv_w[l], conv_b=row(conv_b[l]), conv_ln_g=row(conv_ln_g[l]), conv_ln_b=row(conv_ln_b[l]),
            w_out=w_out[l].astype(BF16), norm_ffn_g=row(norm_ffn_g[l]),
            w_gate=w_gate[l].astype(BF16), w_up=w_up[l].astype(BF16), w_down=w_down[l].astype(BF16),
        )
        y_prompt = _layer(y_prompt, p)
        y_sample = _layer(y_sample, p)
    return (y_prompt, y_sample)
```

```python
import functools

import numpy as np
import jax
import jax.numpy as jnp
from jax import lax
from jax.experimental import pallas as pl
from jax.experimental.pallas import tpu as pltpu

F32 = jnp.float32
BF16 = jnp.bfloat16

HEAD_DIM = 128
LANES = 128
CONV_TAPS = 31
CONV_PAD = (CONV_TAPS - 1) // 2
CONV_HALO = 16
DILATIONS = (1, 4, 16)
BAND = 64
QBLK = 128
KWIN = QBLK + 2 * BAND
GROUP = 16
ROPE_THETA = 10000.0
EPS = 1e-6
NEG = -1e30
VMEM_LIMIT_BYTES = 56 * 1024 * 1024
W_TILE = 512


def _params(semantics):
    return pltpu.CompilerParams(dimension_semantics=semantics, vmem_limit_bytes=VMEM_LIMIT_BYTES)


def _inproj_kernel(x_ref, g_ref, w_ref, cos_ref, sin_ref, qg_ref, kg_ref, qkv_ref, u_ref, hn_ref,
                   *, n_q, n_qkv, scale, norm_rows):
    j = pl.program_id(1)
    tm, tn = u_ref.shape

    def proj(t):
        return jnp.dot(hn_ref[...], w_ref[t], preferred_element_type=F32)

    @pl.when(j == 0)
    def _():
        def body(t, c):
            rows = pl.ds(pl.multiple_of(t * norm_rows, norm_rows), norm_rows)
            x = x_ref[rows, :]
            ms = jnp.mean(x * x, axis=-1, keepdims=True)
            hn_ref[rows, :] = (x * lax.rsqrt(ms + EPS) * g_ref[...]).astype(BF16)
            return c
        lax.fori_loop(0, tm // norm_rows, body, 0, unroll=2)

    @pl.when(j < 2 * n_q)
    def _():
        gain = jnp.where(j < n_q, qg_ref[...] * scale, kg_ref[...])
        cos = cos_ref[...]
        sin = sin_ref[...]
        for t in range(2):
            acc = proj(t)
            heads = [acc[:, h * HEAD_DIM:(h + 1) * HEAD_DIM] for h in range(tn // HEAD_DIM)]
            ms = [jnp.mean(xh * xh, axis=-1, keepdims=True) for xh in heads]
            ys = [xh * lax.rsqrt(m + EPS) * gain for xh, m in zip(heads, ms)]
            rot = [pltpu.roll(y, HEAD_DIM // 2, 1) for y in ys]
            for h, (y, yr) in enumerate(zip(ys, rot)):
                c0 = t * tn + h * HEAD_DIM
                qkv_ref[:, c0:c0 + HEAD_DIM] = (y * cos + yr * sin).astype(BF16)

    @pl.when((j >= 2 * n_q) & (j < n_qkv))
    def _():
        for t in range(2):
            qkv_ref[:, t * tn:(t + 1) * tn] = proj(t).astype(BF16)

    @pl.when(j >= n_qkv)
    def _():
        u_ref[...] = (proj(0) * jax.nn.sigmoid(proj(1))).astype(BF16)


def _in_projection(x2, norm_g, w_in_t, cos_t, sin_t, qg, kg, *, seq, attn_w, conv_w, tm=512):
    T, D = x2.shape
    n_steps, _, _, tn = w_in_t.shape
    n_q = attn_w // (2 * tn)
    n_qkv = 3 * n_q
    assert n_steps == n_qkv + conv_w // tn
    kern = functools.partial(_inproj_kernel, n_q=n_q, n_qkv=n_qkv, scale=HEAD_DIM ** -0.5, norm_rows=16)
    return pl.pallas_call(
        kern,
        grid=(T // tm, n_steps),
        in_specs=[
            pl.BlockSpec((tm, D), lambda i, j: (i, 0)),
            pl.BlockSpec((1, D), lambda i, j: (0, 0)),
            pl.BlockSpec((None, 2, D, tn), lambda i, j: (j, 0, 0, 0)),
            pl.BlockSpec((tm, HEAD_DIM), lambda i, j: (i % (seq // tm), 0)),
            pl.BlockSpec((tm, HEAD_DIM), lambda i, j: (i % (seq // tm), 0)),
            pl.BlockSpec((1, HEAD_DIM), lambda i, j: (0, 0)),
            pl.BlockSpec((1, HEAD_DIM), lambda i, j: (0, 0)),
        ],
        out_specs=[
            pl.BlockSpec((tm, 2 * tn), lambda i, j: (i, jnp.minimum(j, n_qkv - 1))),
            pl.BlockSpec((tm, tn), lambda i, j: (i, jnp.maximum(j - n_qkv, 0))),
        ],
        out_shape=[
            jax.ShapeDtypeStruct((T, 3 * attn_w), BF16),
            jax.ShapeDtypeStruct((T, conv_w), BF16),
        ],
        scratch_shapes=[pltpu.VMEM((tm, D), BF16)],
        compiler_params=_params(("parallel", "arbitrary")),
        name="in_projection",
    )(x2, norm_g, w_in_t, cos_t, sin_t, qg, kg)


def _in_tile_order(attn_w, conv_w, tn):
    n_qkv = 3 * attn_w // tn
    n_c = conv_w // tn
    order = list(range(n_qkv))
    for c in range(n_c):
        order += [n_qkv + c, n_qkv + n_c + c]
    return order


def _softmax_blocks(q, k, v, bias):
    s = jnp.einsum("gqd,gkd->gqk", q, k, preferred_element_type=F32) + bias
    m = jnp.max(s, axis=-1, keepdims=True)
    p = jnp.exp(s - m)
    l = jnp.sum(p, axis=-1, keepdims=True)
    o = jnp.einsum("gqk,gkd->gqd", p.astype(BF16), v, preferred_element_type=F32) * (1.0 / l)
    return o, m + jnp.log(l)


def _attn_kernel(q_ref, k_ref, v_ref, bias1_ref, bias4_ref, bias16_ref, o_ref,
                 f32buf, f32buf4, qp4, kp4, vp4, qp16, kp16, vp16, ob1, ob4, ob16, ls1, ls4, ls16,
                 *, seq, heads):
    l4 = seq // 4
    l16 = seq // 16
    nb4 = l4 // QBLK
    ng1 = seq // (GROUP * QBLK)
    grows = GROUP * QBLK
    blocks = lambda x: x.reshape(GROUP, QBLK, HEAD_DIM)

    def put(ob, ls, rows, o, lse):
        ob[rows, :] = o
        ls[rows, :] = jnp.broadcast_to(lse, o.shape)

    for h in range(heads):
        cols = slice(h * HEAD_DIM, (h + 1) * HEAD_DIM)

        for src, p4, p16 in ((q_ref, qp4, qp16), (k_ref, kp4, kp16), (v_ref, vp4, vp16)):
            f32buf[...] = src[0, :, cols].astype(F32)
            for r4 in range(4):
                x4 = f32buf[pl.ds(r4, l4, stride=4), :]
                f32buf4[r4 * l4:(r4 + 1) * l4, :] = x4
                p4[r4 * l4:(r4 + 1) * l4, :] = x4.astype(BF16)
            for r in range(16):
                a, r4 = divmod(r, 4)
                p16[r * l16:(r + 1) * l16, :] = f32buf4[pl.ds(r4 * l4 + a, l16, stride=4), :].astype(BF16)


        def body1(g, c):
            rows = pl.ds(pl.multiple_of(g * grows, grows), grows)
            starts = [pl.multiple_of(jnp.clip((g * GROUP + j) * QBLK - BAND, 0, seq - KWIN), BAND)
                      for j in range(GROUP)]
            k3 = jnp.stack([k_ref[0, pl.ds(ks, KWIN), cols] for ks in starts])
            v3 = jnp.stack([v_ref[0, pl.ds(ks, KWIN), cols] for ks in starts])
            o, lse = _softmax_blocks(blocks(q_ref[0, rows, cols]), k3, v3, bias1_ref[g])
            put(ob1, ls1, rows, o.reshape(grows, HEAD_DIM), lse.reshape(grows, 1))
            return c
        lax.fori_loop(0, ng1, body1, 0)

        def body4(g, c):
            base = pl.multiple_of(g * grows, grows)
            rows = pl.ds(base, grows)
            starts = [(j // nb4) * l4 + min(max((j % nb4) * QBLK - BAND, 0), l4 - KWIN) for j in range(GROUP)]
            k3 = jnp.stack([kp4[pl.ds(base + ks, KWIN), :] for ks in starts])
            v3 = jnp.stack([vp4[pl.ds(base + ks, KWIN), :] for ks in starts])
            o, lse = _softmax_blocks(blocks(qp4[rows, :]), k3, v3, bias4_ref[...])
            put(ob4, ls4, rows, o.reshape(grows, HEAD_DIM), lse.reshape(grows, 1))
            return c
        lax.fori_loop(0, seq // grows, body4, 0)

        def body16(g, c):
            rows = pl.ds(pl.multiple_of(g * grows, grows), grows)
            o, lse = _softmax_blocks(blocks(qp16[rows, :]), blocks(kp16[rows, :]), blocks(vp16[rows, :]),
                                     bias16_ref[...])
            for j in range(GROUP):
                a = g * (GROUP // 4) + j // 4
                put(ob16, ls16, pl.ds((j % 4) * l4 + a, l16, stride=4), o[j], lse[j])
            return c
        lax.fori_loop(0, 16 // GROUP, body16, 0)

        rc = 64
        for r4 in range(4):
            def comb(t, c, r4=r4):
                rows = pl.ds(pl.multiple_of(r4 * l4 + t * rc, rc), rc)
                nat = pl.ds(4 * t * rc + r4, rc, stride=4)
                a1, a4, a16 = ls1[nat, :], ls4[rows, :], ls16[rows, :]
                mx = jnp.maximum(jnp.maximum(a1, a4), a16)
                e1, e4, e16 = jnp.exp(a1 - mx), jnp.exp(a4 - mx), jnp.exp(a16 - mx)
                out = (e1 * ob1[nat, :] + e4 * ob4[rows, :] + e16 * ob16[rows, :]) / (e1 + e4 + e16)
                f32buf[nat, :] = out
                return c
            lax.fori_loop(0, l4 // rc, comb, 0, unroll=2)
        o_ref[0, :, cols] = f32buf[...].astype(BF16)


def _band_biases(seq):
    r = np.arange(QBLK)[:, None]
    c = np.arange(KWIN)[None, :]
    first = np.abs(c - r) <= BAND
    mid = np.abs(c - BAND - r) <= BAND
    last = np.abs(c - (KWIN - QBLK) - r) <= BAND
    neg = lambda valid: np.where(valid, 0.0, NEG).astype(np.float32)
    nb1 = seq // QBLK
    b1 = neg(np.stack([first] + [mid] * (nb1 - 2) + [last])).reshape(nb1 // GROUP, GROUP, QBLK, KWIN)
    nb4 = seq // (4 * QBLK)
    b4 = neg(np.stack(([first] + [mid] * (nb4 - 2) + [last]) * (GROUP // nb4)))
    c16 = np.arange(seq // 16)[None, :]
    r16 = np.arange(seq // 16)[:, None]
    b16 = neg(np.abs(c16 - r16) <= BAND)
    return jnp.asarray(b1), jnp.asarray(b4), jnp.asarray(b16)


def _attention(proj3, *, attn_w, heads=2):
    B, S, _ = proj3.shape
    assert S // 16 == QBLK and GROUP % (S // (4 * QBLK)) == 0 and S % (GROUP * QBLK) == 0 and GROUP % 4 == 0
    n_hg = attn_w // (heads * HEAD_DIM)
    hw = heads * HEAD_DIM
    b1, b4, b16 = _band_biases(S)
    kern = functools.partial(_attn_kernel, seq=S, heads=heads)
    perm = lambda: pltpu.VMEM((S, HEAD_DIM), BF16)
    nat = lambda: pltpu.VMEM((S, HEAD_DIM), F32)
    return pl.pallas_call(
        kern,
        grid=(B, n_hg),
        in_specs=[
            pl.BlockSpec((1, S, hw), lambda b, g: (b, 0, g)),
            pl.BlockSpec((1, S, hw), lambda b, g: (b, 0, n_hg + g)),
            pl.BlockSpec((1, S, hw), lambda b, g: (b, 0, 2 * n_hg + g)),
            pl.BlockSpec((S // (GROUP * QBLK), GROUP, QBLK, KWIN), lambda b, g: (0, 0, 0, 0)),
            pl.BlockSpec((GROUP, QBLK, KWIN), lambda b, g: (0, 0, 0)),
            pl.BlockSpec((S // 16, S // 16), lambda b, g: (0, 0)),
        ],
        out_specs=pl.BlockSpec((1, S, hw), lambda b, g: (b, 0, g)),
        out_shape=jax.ShapeDtypeStruct((B, S, attn_w), BF16),
        scratch_shapes=[nat(), nat()] + [perm() for _ in range(6)] + [nat() for _ in range(6)],
        compiler_params=_params(("parallel", "parallel")),
        name="dilated_attention",
    )(proj3, proj3, proj3, b1, b4, b16)


def _conv_kernel(prev_ref, main_ref, next_ref, w_ref, b_ref, lg_ref, lb_ref, o_ref, ubuf, ybuf,
                 *, tiles_per_seq, row_blk, norm_rows):
    tt, C = main_ref.shape
    n_cb, _, col_blk = ybuf.shape
    pos = pl.program_id(0) % tiles_per_seq
    zero_halo = jnp.zeros((CONV_HALO, col_blk), F32)
    for cb in range(n_cb):
        cols = slice(cb * col_blk, (cb + 1) * col_blk)
        ubuf[cb, 0:CONV_HALO, :] = jnp.where(pos != 0, prev_ref[:, cols].astype(F32), zero_halo)
        ubuf[cb, CONV_HALO:CONV_HALO + tt, :] = main_ref[:, cols].astype(F32)
        ubuf[cb, CONV_HALO + tt:, :] = jnp.where(pos != tiles_per_seq - 1, next_ref[:, cols].astype(F32),
                                                 zero_halo)

    def conv_body(cb, c):
        for rb in range(tt // (2 * row_blk)):
            for parity in range(2):
                first = rb * 2 * row_blk + parity
                acc = jnp.zeros((row_blk, col_blk), F32)
                for k in range(CONV_TAPS):
                    r = first + CONV_HALO - CONV_PAD + k
                    acc = acc + ubuf[cb, pl.ds(r, row_blk, stride=2), :] * w_ref[cb, k:k + 1, :]
                ybuf[cb, pl.ds(first, row_blk, stride=2), :] = acc + b_ref[cb]
        return c
    lax.fori_loop(0, n_cb, conv_body, 0)

    def norm_body(rb, c):
        rows = pl.ds(pl.multiple_of(rb * norm_rows, norm_rows), norm_rows)
        ys = [ybuf[cb, rows, :] for cb in range(n_cb)]
        mu = jnp.sum(sum(ys), axis=-1, keepdims=True) * (1.0 / C)
        ycs = [y - mu for y in ys]
        var = jnp.sum(sum(yc * yc for yc in ycs), axis=-1, keepdims=True) * (1.0 / C)
        rstd = lax.rsqrt(var + EPS)
        for cb in range(n_cb):
            cols = slice(cb * col_blk, (cb + 1) * col_blk)
            z = ycs[cb] * rstd * lg_ref[:, cols] + lb_ref[:, cols]
            o_ref[rows, cols] = (z * jax.nn.sigmoid(z)).astype(BF16)
        return c
    lax.fori_loop(0, tt // norm_rows, norm_body, 0, unroll=4)


def _conformer_conv(proj2, conv_w, conv_b, ln_g, ln_b, *, seq, conv_width, tt=256, col_blk=LANES):
    T, W = proj2.shape
    C = conv_width
    n_cb = C // col_blk
    ucol = (W - C) // C
    hb = tt // CONV_HALO
    n_halo = T // CONV_HALO
    w_blk = conv_w.reshape(CONV_TAPS, n_cb, col_blk).transpose(1, 0, 2)
    b_blk = conv_b.reshape(n_cb, 1, col_blk)
    kern = functools.partial(_conv_kernel, tiles_per_seq=seq // tt, row_blk=tt // 2, norm_rows=8)
    vec = lambda: pl.BlockSpec((1, C), lambda i: (0, 0))
    return pl.pallas_call(
        kern,
        grid=(T // tt,),
        in_specs=[
            pl.BlockSpec((CONV_HALO, C), lambda i: (jnp.maximum(i * hb - 1, 0), ucol)),
            pl.BlockSpec((tt, C), lambda i: (i, ucol)),
            pl.BlockSpec((CONV_HALO, C), lambda i: (jnp.minimum((i + 1) * hb, n_halo - 1), ucol)),
            pl.BlockSpec((n_cb, CONV_TAPS, col_blk), lambda i: (0, 0, 0)),
            pl.BlockSpec((n_cb, 1, col_blk), lambda i: (0, 0, 0)),
            vec(), vec(),
        ],
        out_specs=pl.BlockSpec((tt, C), lambda i: (i, 0)),
        out_shape=jax.ShapeDtypeStruct((T, C), BF16),
        scratch_shapes=[pltpu.VMEM((n_cb, tt + 2 * CONV_HALO, col_blk), F32),
                        pltpu.VMEM((n_cb, tt, col_blk), F32)],
        compiler_params=_params(("parallel",)),
        name="conformer_conv",
    )(proj2, proj2, proj2, w_blk, b_blk, ln_g, ln_b)


def _outproj_kernel(attn_ref, conv_ref, w_ref, x_ref, g_ref, h_ref, hb_ref, ssq_ref):
    j = pl.program_id(1)
    A = attn_ref.shape[1]
    acc = jnp.dot(attn_ref[...], w_ref[0:A, :], preferred_element_type=F32)
    acc = acc + jnp.dot(conv_ref[...], w_ref[A:, :], preferred_element_type=F32)
    h = x_ref[...] + acc
    h_ref[...] = h
    hb_ref[...] = (h * g_ref[...]).astype(BF16)
    hh = h * h
    part = hh[:, 0:LANES]
    for c in range(1, h.shape[1] // LANES):
        part = part + hh[:, c * LANES:(c + 1) * LANES]

    @pl.when(j == 0)
    def _():
        ssq_ref[...] = part

    @pl.when(j != 0)
    def _():
        ssq_ref[...] += part


def _out_projection(attn2, conv2, w_out_t, x2, ffn_g, *, tm=1024):
    T, D = x2.shape
    A = attn2.shape[1]
    C = conv2.shape[1]
    n_t, _, tn = w_out_t.shape
    return pl.pallas_call(
        _outproj_kernel,
        grid=(T // tm, n_t),
        in_specs=[
            pl.BlockSpec((tm, A), lambda i, j: (i, 0)),
            pl.BlockSpec((tm, C), lambda i, j: (i, 0)),
            pl.BlockSpec((None, A + C, tn), lambda i, j: (j, 0, 0)),
            pl.BlockSpec((tm, tn), lambda i, j: (i, j)),
            pl.BlockSpec((1, tn), lambda i, j: (0, j)),
        ],
        out_specs=[
            pl.BlockSpec((tm, tn), lambda i, j: (i, j)),
            pl.BlockSpec((tm, tn), lambda i, j: (i, j)),
            pl.BlockSpec((tm, LANES), lambda i, j: (i, 0)),
        ],
        out_shape=[
            jax.ShapeDtypeStruct((T, D), F32),
            jax.ShapeDtypeStruct((T, D), BF16),
            jax.ShapeDtypeStruct((T, LANES), F32),
        ],
        compiler_params=_params(("parallel", "arbitrary")),
        name="out_projection",
    )(attn2, conv2, w_out_t, x2, ffn_g)


def _ffn_kernel(hb_ref, ssq_ref, h_hbm, wg_ref, wu_ref, wd_ref, o_ref, r_ref, sem, *, d_model, last_width):
    i = pl.program_id(0)
    j = pl.program_id(1)
    nj = pl.num_programs(1)
    tm = o_ref.shape[0]
    tf = wg_ref.shape[1]

    def residual_copy():
        return pltpu.make_async_copy(h_hbm.at[pl.ds(pl.multiple_of(i * tm, tm), tm), :], o_ref, sem)

    def hidden(width):
        hb = hb_ref[...]
        r = r_ref[:, :width]
        g = jnp.dot(hb, wg_ref[:, :width], preferred_element_type=F32) * r
        u = jnp.dot(hb, wu_ref[:, :width], preferred_element_type=F32) * r
        return (g * jax.nn.sigmoid(g) * u).astype(BF16)

    def accumulate(a, width):
        o_ref[...] += jnp.dot(a, wd_ref[:width, :], preferred_element_type=F32)

    @pl.when(j == 0)
    def _():
        residual_copy().start()
        ms = jnp.sum(ssq_ref[...], axis=-1, keepdims=True) * (1.0 / d_model)
        r_ref[...] = jnp.broadcast_to(lax.rsqrt(ms + EPS), r_ref.shape)
        a = hidden(tf)
        residual_copy().wait()
        accumulate(a, tf)

    @pl.when((j > 0) & (j < nj - 1))
    def _():
        accumulate(hidden(tf), tf)

    @pl.when(j == nj - 1)
    def _():
        accumulate(hidden(last_width), last_width)


def _ffn(hb, ssq, h, wg_t, wu_t, wd_t, *, d_ff, tm=512):
    T, D = h.shape
    n_t, _, tf = wg_t.shape
    assert n_t >= 2
    kern = functools.partial(_ffn_kernel, d_model=D, last_width=d_ff - (n_t - 1) * tf)
    return pl.pallas_call(
        kern,
        grid=(T // tm, n_t),
        in_specs=[
            pl.BlockSpec((tm, D), lambda i, j: (i, 0)),
            pl.BlockSpec((tm, LANES), lambda i, j: (i, 0)),
            pl.BlockSpec(memory_space=pl.ANY),
            pl.BlockSpec((None, D, tf), lambda i, j: (j, 0, 0)),
            pl.BlockSpec((None, D, tf), lambda i, j: (j, 0, 0)),
            pl.BlockSpec((None, tf, D), lambda i, j: (j, 0, 0)),
        ],
        out_specs=pl.BlockSpec((tm, D), lambda i, j: (i, 0)),
        out_shape=jax.ShapeDtypeStruct((T, D), F32),
        scratch_shapes=[pltpu.VMEM((tm, tf), F32), pltpu.SemaphoreType.DMA(())],
        compiler_params=_params(("parallel", "arbitrary")),
        name="swiglu_ffn",
    )(hb, ssq, h, wg_t, wu_t, wd_t)


def _col_tiles(w, tn, order=None):
    K, N = w.shape
    n_t = -(-N // tn)
    w = jnp.pad(w.astype(BF16), ((0, 0), (0, n_t * tn - N))).reshape(K, n_t, tn)
    if order is not None:
        w = w[:, np.asarray(order), :]
    return w.transpose(1, 0, 2)


def _row_tiles(w, tk):
    K, N = w.shape
    n_t = -(-K // tk)
    return jnp.pad(w.astype(BF16), ((0, n_t * tk - K), (0, 0))).reshape(n_t, tk, N)


def _rope_tables(seq):
    half = HEAD_DIM // 2
    freqs = ROPE_THETA ** (-jnp.arange(half, dtype=F32) * 2.0 / HEAD_DIM)
    ang = jnp.arange(seq, dtype=F32)[:, None] * freqs[None, :]
    cos = jnp.cos(ang)
    sin = jnp.sin(ang)
    return jnp.concatenate([cos, cos], axis=-1), jnp.concatenate([-sin, sin], axis=-1)


def _layer(x, p):
    B, S, D = x.shape
    T = B * S
    attn_w = D // 2
    conv_w = D - attn_w
    x2 = x.reshape(T, D)
    qkv, u = _in_projection(x2, p["norm_mix_g"], p["w_in"], p["cos"], p["sin"], p["q_norm_g"], p["k_norm_g"],
                            seq=S, attn_w=attn_w, conv_w=conv_w)
    attn = _attention(qkv.reshape(B, S, -1), attn_w=attn_w)
    conv = _conformer_conv(u, p["conv_w"], p["conv_b"], p["conv_ln_g"], p["conv_ln_b"],
                           seq=S, conv_width=conv_w)
    h, hb, ssq = _out_projection(attn.reshape(T, attn_w), conv, p["w_out"], x2, p["norm_ffn_g"])
    y = _ffn(hb, ssq, h, p["w_gate"], p["w_up"], p["w_down"], d_ff=p["d_ff"])
    return y.reshape(B, S, D)


def kernel(x_prompt, x_sample, norm_mix_g, w_in, q_norm_g, k_norm_g, conv_w, conv_b, conv_ln_g, conv_ln_b,
           w_out, norm_ffn_g, w_gate, w_up, w_down):
    depth = w_in.shape[0]
    y_prompt, y_sample = x_prompt, x_sample
    cos, sin = _rope_tables(x_prompt.shape[1])
    assert x_sample.shape[1] == x_prompt.shape[1]
    row = lambda v: v.reshape(1, -1)
    d_model = w_in.shape[1]
    attn_w = d_model // 2
    conv_width = d_model - attn_w
    for l in range(depth):
        w_in_t = _col_tiles(w_in[l], W_TILE, _in_tile_order(attn_w, conv_width, W_TILE))
        p = dict(
            norm_mix_g=row(norm_mix_g[l]), w_in=w_in_t.reshape(-1, 2, d_model, W_TILE), cos=cos, sin=sin,
            q_norm_g=row(q_norm_g[l]), k_norm_g=row(k_norm_g[l]),
            conv_w=conv_w[l], conv_b=row(conv_b[l]), conv_ln_g=row(conv_ln_g[l]), conv_ln_b=row(conv_ln_b[l]),
            w_out=_col_tiles(w_out[l], W_TILE), norm_ffn_g=row(norm_ffn_g[l]),
            w_gate=_col_tiles(w_gate[l], W_TILE), w_up=_col_tiles(w_up[l], W_TILE),
            w_down=_row_tiles(w_down[l], W_TILE), d_ff=w_gate.shape[2],
        )
        y_prompt = _layer(y_prompt, p)
        y_sample = _layer(y_sample, p)
    return (y_prompt, y_sample)
```

```python
import functools

import numpy as np
import jax
import jax.numpy as jnp
from jax import lax
from jax.experimental import pallas as pl
from jax.experimental.pallas import tpu as pltpu

F32 = jnp.float32
BF16 = jnp.bfloat16

HEAD_DIM = 128
LANES = 128
CONV_TAPS = 31
CONV_PAD = (CONV_TAPS - 1) // 2
CONV_HALO = 16
DILATIONS = (1, 4, 16)
BAND = 64
QBLK = 128
KWIN = QBLK + 2 * BAND
GROUP = 16
ROPE_THETA = 10000.0
EPS = 1e-6
NEG = -1e30
VMEM_LIMIT_BYTES = 56 * 1024 * 1024
W_TILE = 512


def _params(semantics):
    return pltpu.CompilerParams(dimension_semantics=semantics, vmem_limit_bytes=VMEM_LIMIT_BYTES)


def _inproj_kernel(x_ref, g_ref, wa_ref, wb_ref, cos_ref, sin_ref, qg_ref, kg_ref, qkv_ref, u_ref, hn_ref,
                   *, n_q, n_qkv, scale, norm_rows):
    j = pl.program_id(1)
    tm, tn = u_ref.shape

    def proj(t):
        w_ref = (wa_ref, wb_ref)[t]
        return jnp.dot(hn_ref[...], w_ref[...], preferred_element_type=F32)

    @pl.when(j == 0)
    def _():
        def body(t, c):
            rows = pl.ds(pl.multiple_of(t * norm_rows, norm_rows), norm_rows)
            x = x_ref[rows, :]
            ms = jnp.mean(x * x, axis=-1, keepdims=True)
            hn_ref[rows, :] = (x * lax.rsqrt(ms + EPS) * g_ref[...]).astype(BF16)
            return c
        lax.fori_loop(0, tm // norm_rows, body, 0, unroll=2)

    @pl.when(j < 2 * n_q)
    def _():
        gain = jnp.where(j < n_q, qg_ref[...] * scale, kg_ref[...])
        cos = cos_ref[...]
        sin = sin_ref[...]
        for t in range(2):
            acc = proj(t)
            heads = [acc[:, h * HEAD_DIM:(h + 1) * HEAD_DIM] for h in range(tn // HEAD_DIM)]
            ms = [jnp.mean(xh * xh, axis=-1, keepdims=True) for xh in heads]
            ys = [xh * lax.rsqrt(m + EPS) * gain for xh, m in zip(heads, ms)]
            rot = [pltpu.roll(y, HEAD_DIM // 2, 1) for y in ys]
            for h, (y, yr) in enumerate(zip(ys, rot)):
                c0 = t * tn + h * HEAD_DIM
                qkv_ref[:, c0:c0 + HEAD_DIM] = (y * cos + yr * sin).astype(BF16)

    @pl.when((j >= 2 * n_q) & (j < n_qkv))
    def _():
        for t in range(2):
            qkv_ref[:, t * tn:(t + 1) * tn] = proj(t).astype(BF16)

    @pl.when(j >= n_qkv)
    def _():
        u_ref[...] = (proj(0) * jax.nn.sigmoid(proj(1))).astype(BF16)


def _in_projection(x2, norm_g, w_in_bf, cos_t, sin_t, qg, kg, *, seq, attn_w, conv_w, tm=512, tn=W_TILE):
    T, D = x2.shape
    n_q = attn_w // (2 * tn)
    n_qkv = 3 * n_q
    n_c = conv_w // tn
    n_steps = n_qkv + n_c
    kern = functools.partial(_inproj_kernel, n_q=n_q, n_qkv=n_qkv, scale=HEAD_DIM ** -0.5, norm_rows=16)
    return pl.pallas_call(
        kern,
        grid=(T // tm, n_steps),
        in_specs=[
            pl.BlockSpec((tm, D), lambda i, j: (i, 0)),
            pl.BlockSpec((1, D), lambda i, j: (0, 0)),
            pl.BlockSpec((D, tn), lambda i, j: (0, jnp.where(j < n_qkv, 2 * j, j + n_qkv))),
            pl.BlockSpec((D, tn), lambda i, j: (0, jnp.where(j < n_qkv, 2 * j + 1, j + n_qkv + n_c))),
            pl.BlockSpec((tm, HEAD_DIM), lambda i, j: (i % (seq // tm), 0)),
            pl.BlockSpec((tm, HEAD_DIM), lambda i, j: (i % (seq // tm), 0)),
            pl.BlockSpec((1, HEAD_DIM), lambda i, j: (0, 0)),
            pl.BlockSpec((1, HEAD_DIM), lambda i, j: (0, 0)),
        ],
        out_specs=[
            pl.BlockSpec((tm, 2 * tn), lambda i, j: (i, jnp.minimum(j, n_qkv - 1))),
            pl.BlockSpec((tm, tn), lambda i, j: (i, jnp.maximum(j - n_qkv, 0))),
        ],
        out_shape=[
            jax.ShapeDtypeStruct((T, 3 * attn_w), BF16),
            jax.ShapeDtypeStruct((T, conv_w), BF16),
        ],
        scratch_shapes=[pltpu.VMEM((tm, D), BF16)],
        compiler_params=_params(("parallel", "arbitrary")),
        name="in_projection",
    )(x2, norm_g, w_in_bf, w_in_bf, cos_t, sin_t, qg, kg)


def _softmax_blocks(q, k, v, bias):
    s = jnp.einsum("gqd,gkd->gqk", q, k, preferred_element_type=F32) + bias
    m = jnp.max(s, axis=-1, keepdims=True)
    p = jnp.exp(s - m)
    l = jnp.sum(p, axis=-1, keepdims=True)
    o = jnp.einsum("gqk,gkd->gqd", p.astype(BF16), v, preferred_element_type=F32) * (1.0 / l)
    return o, m + jnp.log(l)


def _attn_kernel(q_ref, k_ref, v_ref, bias1_ref, bias4_ref, bias16_ref, o_ref,
                 f32buf, f32buf4, qp4, kp4, vp4, qp16, kp16, vp16, ob1, ob4, ob16, ls1, ls4, ls16,
                 *, seq, heads):
    l4 = seq // 4
    l16 = seq // 16
    nb4 = l4 // QBLK
    ng1 = seq // (GROUP * QBLK)
    grows = GROUP * QBLK
    blocks = lambda x: x.reshape(GROUP, QBLK, HEAD_DIM)

    def put(ob, ls, rows, o, lse):
        ob[rows, :] = o
        ls[rows, :] = jnp.broadcast_to(lse, o.shape)

    for h in range(heads):
        cols = slice(h * HEAD_DIM, (h + 1) * HEAD_DIM)

        for src, p4, p16 in ((q_ref, qp4, qp16), (k_ref, kp4, kp16), (v_ref, vp4, vp16)):
            f32buf[...] = src[0, :, cols].astype(F32)
            for r4 in range(4):
                x4 = f32buf[pl.ds(r4, l4, stride=4), :]
                f32buf4[r4 * l4:(r4 + 1) * l4, :] = x4
                p4[r4 * l4:(r4 + 1) * l4, :] = x4.astype(BF16)
            for r in range(16):
                a, r4 = divmod(r, 4)
                p16[r * l16:(r + 1) * l16, :] = f32buf4[pl.ds(r4 * l4 + a, l16, stride=4), :].astype(BF16)


        def body1(g, c):
            rows = pl.ds(pl.multiple_of(g * grows, grows), grows)
            starts = [pl.multiple_of(jnp.clip((g * GROUP + j) * QBLK - BAND, 0, seq - KWIN), BAND)
                      for j in range(GROUP)]
            k3 = jnp.stack([k_ref[0, pl.ds(ks, KWIN), cols] for ks in starts])
            v3 = jnp.stack([v_ref[0, pl.ds(ks, KWIN), cols] for ks in starts])
            o, lse = _softmax_blocks(blocks(q_ref[0, rows, cols]), k3, v3, bias1_ref[g])
            put(ob1, ls1, rows, o.reshape(grows, HEAD_DIM), lse.reshape(grows, 1))
            return c
        lax.fori_loop(0, ng1, body1, 0)

        def body4(g, c):
            base = pl.multiple_of(g * grows, grows)
            rows = pl.ds(base, grows)
            starts = [(j // nb4) * l4 + min(max((j % nb4) * QBLK - BAND, 0), l4 - KWIN) for j in range(GROUP)]
            k3 = jnp.stack([kp4[pl.ds(base + ks, KWIN), :] for ks in starts])
            v3 = jnp.stack([vp4[pl.ds(base + ks, KWIN), :] for ks in starts])
            o, lse = _softmax_blocks(blocks(qp4[rows, :]), k3, v3, bias4_ref[...])
            put(ob4, ls4, rows, o.reshape(grows, HEAD_DIM), lse.reshape(grows, 1))
            return c
        lax.fori_loop(0, seq // grows, body4, 0)

        def body16(g, c):
            rows = pl.ds(pl.multiple_of(g * grows, grows), grows)
            o, lse = _softmax_blocks(blocks(qp16[rows, :]), blocks(kp16[rows, :]), blocks(vp16[rows, :]),
                                     bias16_ref[...])
            for j in range(GROUP):
                a = g * (GROUP // 4) + j // 4
                put(ob16, ls16, pl.ds((j % 4) * l4 + a, l16, stride=4), o[j], lse[j])
            return c
        lax.fori_loop(0, 16 // GROUP, body16, 0)

        rc = 64
        for r4 in range(4):
            def comb(t, c, r4=r4):
                rows = pl.ds(pl.multiple_of(r4 * l4 + t * rc, rc), rc)
                nat = pl.ds(4 * t * rc + r4, rc, stride=4)
                a1, a4, a16 = ls1[nat, :], ls4[rows, :], ls16[rows, :]
                mx = jnp.maximum(jnp.maximum(a1, a4), a16)
                e1, e4, e16 = jnp.exp(a1 - mx), jnp.exp(a4 - mx), jnp.exp(a16 - mx)
                out = (e1 * ob1[nat, :] + e4 * ob4[rows, :] + e16 * ob16[rows, :]) / (e1 + e4 + e16)
                f32buf[nat, :] = out
                return c
            lax.fori_loop(0, l4 // rc, comb, 0, unroll=2)
        o_ref[0, :, cols] = f32buf[...].astype(BF16)


def _band_biases(seq):
    r = np.arange(QBLK)[:, None]
    c = np.arange(KWIN)[None, :]
    first = np.abs(c - r) <= BAND
    mid = np.abs(c - BAND - r) <= BAND
    last = np.abs(c - (KWIN - QBLK) - r) <= BAND
    neg = lambda valid: np.where(valid, 0.0, NEG).astype(np.float32)
    nb1 = seq // QBLK
    b1 = neg(np.stack([first] + [mid] * (nb1 - 2) + [last])).reshape(nb1 // GROUP, GROUP, QBLK, KWIN)
    nb4 = seq // (4 * QBLK)
    b4 = neg(np.stack(([first] + [mid] * (nb4 - 2) + [last]) * (GROUP // nb4)))
    c16 = np.arange(seq // 16)[None, :]
    r16 = np.arange(seq // 16)[:, None]
    b16 = neg(np.abs(c16 - r16) <= BAND)
    return jnp.asarray(b1), jnp.asarray(b4), jnp.asarray(b16)


def _attention(proj3, *, attn_w, heads=2):
    B, S, _ = proj3.shape
    assert S // 16 == QBLK and GROUP % (S // (4 * QBLK)) == 0 and S % (GROUP * QBLK) == 0 and GROUP % 4 == 0
    n_hg = attn_w // (heads * HEAD_DIM)
    hw = heads * HEAD_DIM
    b1, b4, b16 = _band_biases(S)
    kern = functools.partial(_attn_kernel, seq=S, heads=heads)
    perm = lambda: pltpu.VMEM((S, HEAD_DIM), BF16)
    nat = lambda: pltpu.VMEM((S, HEAD_DIM), F32)
    return pl.pallas_call(
        kern,
        grid=(B, n_hg),
        in_specs=[
            pl.BlockSpec((1, S, hw), lambda b, g: (b, 0, g)),
            pl.BlockSpec((1, S, hw), lambda b, g: (b, 0, n_hg + g)),
            pl.BlockSpec((1, S, hw), lambda b, g: (b, 0, 2 * n_hg + g)),
            pl.BlockSpec((S // (GROUP * QBLK), GROUP, QBLK, KWIN), lambda b, g: (0, 0, 0, 0)),
            pl.BlockSpec((GROUP, QBLK, KWIN), lambda b, g: (0, 0, 0)),
            pl.BlockSpec((S // 16, S // 16), lambda b, g: (0, 0)),
        ],
        out_specs=pl.BlockSpec((1, S, hw), lambda b, g: (b, 0, g)),
        out_shape=jax.ShapeDtypeStruct((B, S, attn_w), BF16),
        scratch_shapes=[nat(), nat()] + [perm() for _ in range(6)] + [nat() for _ in range(6)],
        compiler_params=_params(("parallel", "parallel")),
        name="dilated_attention",
    )(proj3, proj3, proj3, b1, b4, b16)


def _conv_kernel(prev_ref, main_ref, next_ref, w_ref, b_ref, lg_ref, lb_ref, o_ref, ubuf, ybuf,
                 *, tiles_per_seq, row_blk, norm_rows):
    tt, C = main_ref.shape
    n_cb, _, col_blk = ybuf.shape
    pos = pl.program_id(0) % tiles_per_seq
    zero_halo = jnp.zeros((CONV_HALO, col_blk), F32)
    for cb in range(n_cb):
        cols = slice(cb * col_blk, (cb + 1) * col_blk)
        ubuf[cb, 0:CONV_HALO, :] = jnp.where(pos != 0, prev_ref[:, cols].astype(F32), zero_halo)
        ubuf[cb, CONV_HALO:CONV_HALO + tt, :] = main_ref[:, cols].astype(F32)
        ubuf[cb, CONV_HALO + tt:, :] = jnp.where(pos != tiles_per_seq - 1, next_ref[:, cols].astype(F32),
                                                 zero_halo)

    def conv_body(cb, c):
        for rb in range(tt // (2 * row_blk)):
            for parity in range(2):
                first = rb * 2 * row_blk + parity
                acc = jnp.zeros((row_blk, col_blk), F32)
                for k in range(CONV_TAPS):
                    r = first + CONV_HALO - CONV_PAD + k
                    acc = acc + ubuf[cb, pl.ds(r, row_blk, stride=2), :] * w_ref[cb, k:k + 1, :]
                ybuf[cb, pl.ds(first, row_blk, stride=2), :] = acc + b_ref[cb]
        return c
    lax.fori_loop(0, n_cb, conv_body, 0)

    def norm_body(rb, c):
        rows = pl.ds(pl.multiple_of(rb * norm_rows, norm_rows), norm_rows)
        ys = [ybuf[cb, rows, :] for cb in range(n_cb)]
        mu = jnp.sum(sum(ys), axis=-1, keepdims=True) * (1.0 / C)
        ycs = [y - mu for y in ys]
        var = jnp.sum(sum(yc * yc for yc in ycs), axis=-1, keepdims=True) * (1.0 / C)
        rstd = lax.rsqrt(var + EPS)
        for cb in range(n_cb):
            cols = slice(cb * col_blk, (cb + 1) * col_blk)
            z = ycs[cb] * rstd * lg_ref[:, cols] + lb_ref[:, cols]
            o_ref[rows, cols] = (z * jax.nn.sigmoid(z)).astype(BF16)
        return c
    lax.fori_loop(0, tt // norm_rows, norm_body, 0, unroll=2)


def _conformer_conv(proj2, conv_w, conv_b, ln_g, ln_b, *, seq, conv_width, tt=256, col_blk=LANES):
    T, W = proj2.shape
    C = conv_width
    n_cb = C // col_blk
    ucol = (W - C) // C
    hb = tt // CONV_HALO
    n_halo = T // CONV_HALO
    w_blk = conv_w.reshape(CONV_TAPS, n_cb, col_blk).transpose(1, 0, 2)
    b_blk = conv_b.reshape(n_cb, 1, col_blk)
    kern = functools.partial(_conv_kernel, tiles_per_seq=seq // tt, row_blk=tt // 2, norm_rows=32)
    vec = lambda: pl.BlockSpec((1, C), lambda i: (0, 0))
    return pl.pallas_call(
        kern,
        grid=(T // tt,),
        in_specs=[
            pl.BlockSpec((CONV_HALO, C), lambda i: (jnp.maximum(i * hb - 1, 0), ucol)),
            pl.BlockSpec((tt, C), lambda i: (i, ucol)),
            pl.BlockSpec((CONV_HALO, C), lambda i: (jnp.minimum((i + 1) * hb, n_halo - 1), ucol)),
            pl.BlockSpec((n_cb, CONV_TAPS, col_blk), lambda i: (0, 0, 0)),
            pl.BlockSpec((n_cb, 1, col_blk), lambda i: (0, 0, 0)),
            vec(), vec(),
        ],
        out_specs=pl.BlockSpec((tt, C), lambda i: (i, 0)),
        out_shape=jax.ShapeDtypeStruct((T, C), BF16),
        scratch_shapes=[pltpu.VMEM((n_cb, tt + 2 * CONV_HALO, col_blk), F32),
                        pltpu.VMEM((n_cb, tt, col_blk), F32)],
        compiler_params=_params(("parallel",)),
        name="conformer_conv",
    )(proj2, proj2, proj2, w_blk, b_blk, ln_g, ln_b)


def _outproj_kernel(attn_ref, conv_ref, w_ref, x_ref, g_ref, h_ref, hb_ref, ssq_ref):
    j = pl.program_id(1)
    A = attn_ref.shape[1]
    acc = jnp.dot(attn_ref[...], w_ref[0:A, :], preferred_element_type=F32)
    acc = acc + jnp.dot(conv_ref[...], w_ref[A:, :], preferred_element_type=F32)
    h = x_ref[...] + acc
    h_ref[...] = h
    hb_ref[...] = (h * g_ref[...]).astype(BF16)
    hh = h * h
    part = hh[:, 0:LANES]
    for c in range(1, h.shape[1] // LANES):
        part = part + hh[:, c * LANES:(c + 1) * LANES]

    @pl.when(j == 0)
    def _():
        ssq_ref[...] = part

    @pl.when(j != 0)
    def _():
        ssq_ref[...] += part


def _out_projection(attn2, conv2, w_out_bf, x2, ffn_g, *, tm=1024, tn=W_TILE):
    T, D = x2.shape
    A = attn2.shape[1]
    C = conv2.shape[1]
    return pl.pallas_call(
        _outproj_kernel,
        grid=(T // tm, D // tn),
        in_specs=[
            pl.BlockSpec((tm, A), lambda i, j: (i, 0)),
            pl.BlockSpec((tm, C), lambda i, j: (i, 0)),
            pl.BlockSpec((A + C, tn), lambda i, j: (0, j)),
            pl.BlockSpec((tm, tn), lambda i, j: (i, j)),
            pl.BlockSpec((1, tn), lambda i, j: (0, j)),
        ],
        out_specs=[
            pl.BlockSpec((tm, tn), lambda i, j: (i, j)),
            pl.BlockSpec((tm, tn), lambda i, j: (i, j)),
            pl.BlockSpec((tm, LANES), lambda i, j: (i, 0)),
        ],
        out_shape=[
            jax.ShapeDtypeStruct((T, D), F32),
            jax.ShapeDtypeStruct((T, D), BF16),
            jax.ShapeDtypeStruct((T, LANES), F32),
        ],
        compiler_params=_params(("parallel", "arbitrary")),
        name="out_projection",
    )(attn2, conv2, w_out_bf, x2, ffn_g)


def _ffn_kernel(hb_ref, ssq_ref, h_hbm, wg_ref, wu_ref, wd_ref, o_ref, r_ref, sem, *, d_model, last_width):
    i = pl.program_id(0)
    j = pl.program_id(1)
    nj = pl.num_programs(1)
    tm = o_ref.shape[0]
    tf = wg_ref.shape[1]

    def residual_copy():
        return pltpu.make_async_copy(h_hbm.at[pl.ds(pl.multiple_of(i * tm, tm), tm), :], o_ref, sem)

    def hidden(width):
        hb = hb_ref[...]
        r = r_ref[:, :width]
        g = jnp.dot(hb, wg_ref[:, :width], preferred_element_type=F32) * r
        u = jnp.dot(hb, wu_ref[:, :width], preferred_element_type=F32) * r
        return (g * jax.nn.sigmoid(g) * u).astype(BF16)

    def accumulate(a, width):
        o_ref[...] += jnp.dot(a, wd_ref[:width, :], preferred_element_type=F32)

    @pl.when(j == 0)
    def _():
        residual_copy().start()
        ms = jnp.sum(ssq_ref[...], axis=-1, keepdims=True) * (1.0 / d_model)
        r_ref[...] = jnp.broadcast_to(lax.rsqrt(ms + EPS), r_ref.shape)
        a = hidden(tf)
        residual_copy().wait()
        accumulate(a, tf)

    @pl.when((j > 0) & (j < nj - 1))
    def _():
        accumulate(hidden(tf), tf)

    @pl.when(j == nj - 1)
    def _():
        accumulate(hidden(last_width), last_width)


def _ffn(hb, ssq, h, wg_bf, wu_bf, wd_bf, *, tm=512, tf=W_TILE):
    T, D = h.shape
    d_ff = wg_bf.shape[1]
    n_t = pl.cdiv(d_ff, tf)
    assert n_t >= 2
    kern = functools.partial(_ffn_kernel, d_model=D, last_width=d_ff - (n_t - 1) * tf)
    return pl.pallas_call(
        kern,
        grid=(T // tm, n_t),
        in_specs=[
            pl.BlockSpec((tm, D), lambda i, j: (i, 0)),
            pl.BlockSpec((tm, LANES), lambda i, j: (i, 0)),
            pl.BlockSpec(memory_space=pl.ANY),
            pl.BlockSpec((D, tf), lambda i, j: (0, j)),
            pl.BlockSpec((D, tf), lambda i, j: (0, j)),
            pl.BlockSpec((tf, D), lambda i, j: (j, 0)),
        ],
        out_specs=pl.BlockSpec((tm, D), lambda i, j: (i, 0)),
        out_shape=jax.ShapeDtypeStruct((T, D), F32),
        scratch_shapes=[pltpu.VMEM((tm, tf), F32), pltpu.SemaphoreType.DMA(())],
        compiler_params=_params(("parallel", "arbitrary")),
        name="swiglu_ffn",
    )(hb, ssq, h, wg_bf, wu_bf, wd_bf)


def _rope_tables(seq):
    half = HEAD_DIM // 2
    freqs = ROPE_THETA ** (-jnp.arange(half, dtype=F32) * 2.0 / HEAD_DIM)
    ang = jnp.arange(seq, dtype=F32)[:, None] * freqs[None, :]
    cos = jnp.cos(ang)
    sin = jnp.sin(ang)
    return jnp.concatenate([cos, cos], axis=-1), jnp.concatenate([-sin, sin], axis=-1)


def _layer(x, p):
    B, S, D = x.shape
    T = B * S
    attn_w = D // 2
    conv_w = D - attn_w
    x2 = x.reshape(T, D)
    qkv, u = _in_projection(x2, p["norm_mix_g"], p["w_in"], p["cos"], p["sin"], p["q_norm_g"], p["k_norm_g"],
                            seq=S, attn_w=attn_w, conv_w=conv_w)
    attn = _attention(qkv.reshape(B, S, -1), attn_w=attn_w)
    conv = _conformer_conv(u, p["conv_w"], p["conv_b"], p["conv_ln_g"], p["conv_ln_b"],
                           seq=S, conv_width=conv_w)
    h, hb, ssq = _out_projection(attn.reshape(T, attn_w), conv, p["w_out"], x2, p["norm_ffn_g"])
    y = _ffn(hb, ssq, h, p["w_gate"], p["w_up"], p["w_down"])
    return y.reshape(B, S, D)


def kernel(x_prompt, x_sample, norm_mix_g, w_in, q_norm_g, k_norm_g, conv_w, conv_b, conv_ln_g, conv_ln_b,
           w_out, norm_ffn_g, w_gate, w_up, w_down):
    depth = w_in.shape[0]
    y_prompt, y_sample = x_prompt, x_sample
    cos, sin = _rope_tables(x_prompt.shape[1])
    assert x_sample.shape[1] == x_prompt.shape[1]
    row = lambda v: v.reshape(1, -1)
    for l in range(depth):
        p = dict(
            norm_mix_g=row(norm_mix_g[l]), w_in=w_in[l].astype(BF16), cos=cos, sin=sin,
            q_norm_g=row(q_norm_g[l]), k_norm_g=row(k_norm_g[l]),
            conv_w=conv_w[l], conv_b=row(conv_b[l]), conv_ln_g=row(conv_ln_g[l]), conv_ln_b=row(conv_ln_b[l]),
            w_out=w_out[l].astype(BF16), norm_ffn_g=row(norm_ffn_g[l]),
            w_gate=w_gate[l].astype(BF16), w_up=w_up[l].astype(BF16), w_down=w_down[l].astype(BF16),
        )
        y_prompt = _layer(y_prompt, p)
        y_sample = _layer(y_sample, p)
    return (y_prompt, y_sample)
```

```python
import functools

import numpy as np
import jax
import jax.numpy as jnp
from jax import lax
from jax.experimental import pallas as pl
from jax.experimental.pallas import tpu as pltpu

F32 = jnp.float32
BF16 = jnp.bfloat16

HEAD_DIM = 128
LANES = 128
CONV_TAPS = 31
CONV_PAD = (CONV_TAPS - 1) // 2
CONV_HALO = 16
DILATIONS = (1, 4, 16)
BAND = 64
QBLK = 128
KWIN = QBLK + 2 * BAND
GROUP = 16
ROPE_THETA = 10000.0
EPS = 1e-6
NEG = -1e30
VMEM_LIMIT_BYTES = 56 * 1024 * 1024
FFN_VMEM_LIMIT_BYTES = 60 * 1024 * 1024
W_TILE = 512


def _params(semantics, vmem_limit_bytes=VMEM_LIMIT_BYTES):
    return pltpu.CompilerParams(dimension_semantics=semantics, vmem_limit_bytes=vmem_limit_bytes)


def _inproj_kernel(x_ref, g_ref, wa_ref, wb_ref, cos_ref, sin_ref, qg_ref, kg_ref, qkv_ref, u_ref, hn_ref,
                   *, n_q, n_qkv, scale, norm_rows):
    j = pl.program_id(1)
    tm, tn = u_ref.shape

    def proj(t):
        w_ref = (wa_ref, wb_ref)[t]
        return jnp.dot(hn_ref[...], w_ref[...], preferred_element_type=F32)

    @pl.when(j == 0)
    def _():
        def body(t, c):
            rows = pl.ds(pl.multiple_of(t * norm_rows, norm_rows), norm_rows)
            x = x_ref[rows, :]
            ms = jnp.mean(x * x, axis=-1, keepdims=True)
            hn_ref[rows, :] = (x * lax.rsqrt(ms + EPS) * g_ref[...]).astype(BF16)
            return c
        lax.fori_loop(0, tm // norm_rows, body, 0, unroll=2)

    @pl.when(j < 2 * n_q)
    def _():
        gain = jnp.where(j < n_q, qg_ref[...] * scale, kg_ref[...])
        cos = cos_ref[...]
        sin = sin_ref[...]
        for t in range(2):
            acc = proj(t)
            heads = [acc[:, h * HEAD_DIM:(h + 1) * HEAD_DIM] for h in range(tn // HEAD_DIM)]
            ms = [jnp.mean(xh * xh, axis=-1, keepdims=True) for xh in heads]
            ys = [xh * lax.rsqrt(m + EPS) * gain for xh, m in zip(heads, ms)]
            rot = [pltpu.roll(y, HEAD_DIM // 2, 1) for y in ys]
            for h, (y, yr) in enumerate(zip(ys, rot)):
                c0 = t * tn + h * HEAD_DIM
                qkv_ref[:, c0:c0 + HEAD_DIM] = (y * cos + yr * sin).astype(BF16)

    @pl.when((j >= 2 * n_q) & (j < n_qkv))
    def _():
        for t in range(2):
            qkv_ref[:, t * tn:(t + 1) * tn] = proj(t).astype(BF16)

    @pl.when(j >= n_qkv)
    def _():
        u_ref[...] = (proj(0) * jax.nn.sigmoid(proj(1))).astype(BF16)


def _in_projection(x2, norm_g, w_in_bf, cos_t, sin_t, qg, kg, *, seq, attn_w, conv_w, tm=512, tn=W_TILE):
    T, D = x2.shape
    n_q = attn_w // (2 * tn)
    n_qkv = 3 * n_q
    n_c = conv_w // tn
    n_steps = n_qkv + n_c
    kern = functools.partial(_inproj_kernel, n_q=n_q, n_qkv=n_qkv, scale=HEAD_DIM ** -0.5, norm_rows=16)
    return pl.pallas_call(
        kern,
        grid=(T // tm, n_steps),
        in_specs=[
            pl.BlockSpec((tm, D), lambda i, j: (i, 0)),
            pl.BlockSpec((1, D), lambda i, j: (0, 0)),
            pl.BlockSpec((D, tn), lambda i, j: (0, jnp.where(j < n_qkv, 2 * j, j + n_qkv))),
            pl.BlockSpec((D, tn), lambda i, j: (0, jnp.where(j < n_qkv, 2 * j + 1, j + n_qkv + n_c))),
            pl.BlockSpec((tm, HEAD_DIM), lambda i, j: (i % (seq // tm), 0)),
            pl.BlockSpec((tm, HEAD_DIM), lambda i, j: (i % (seq // tm), 0)),
            pl.BlockSpec((1, HEAD_DIM), lambda i, j: (0, 0)),
            pl.BlockSpec((1, HEAD_DIM), lambda i, j: (0, 0)),
        ],
        out_specs=[
            pl.BlockSpec((tm, 2 * tn), lambda i, j: (i, jnp.minimum(j, n_qkv - 1))),
            pl.BlockSpec((tm, tn), lambda i, j: (i, jnp.maximum(j - n_qkv, 0))),
        ],
        out_shape=[
            jax.ShapeDtypeStruct((T, 3 * attn_w), BF16),
            jax.ShapeDtypeStruct((T, conv_w), BF16),
        ],
        scratch_shapes=[pltpu.VMEM((tm, D), BF16)],
        compiler_params=_params(("parallel", "arbitrary")),
        name="in_projection",
    )(x2, norm_g, w_in_bf, w_in_bf, cos_t, sin_t, qg, kg)


def _softmax_blocks(q, k, v, bias):
    s = jnp.einsum("gqd,gkd->gqk", q, k, preferred_element_type=F32) + bias
    m = jnp.max(s, axis=-1, keepdims=True)
    p = jnp.exp(s - m)
    l = jnp.sum(p, axis=-1, keepdims=True)
    o = jnp.einsum("gqk,gkd->gqd", p.astype(BF16), v, preferred_element_type=F32) * (1.0 / l)
    return o, m + jnp.log(l)


def _attn_kernel(q_ref, k_ref, v_ref, bias1_ref, bias4_ref, bias16_ref, o_ref,
                 f32buf, f32buf4, qp4, kp4, vp4, qp16, kp16, vp16, ob1, ob4, ob16, ls1, ls4, ls16,
                 *, seq, heads):
    l4 = seq // 4
    l16 = seq // 16
    nb4 = l4 // QBLK
    ng1 = seq // (GROUP * QBLK)
    grows = GROUP * QBLK
    blocks = lambda x: x.reshape(GROUP, QBLK, HEAD_DIM)

    def put(ob, ls, rows, o, lse):
        ob[rows, :] = o
        ls[rows, :] = jnp.broadcast_to(lse, o.shape)

    for h in range(heads):
        cols = slice(h * HEAD_DIM, (h + 1) * HEAD_DIM)

        for src, p4, p16 in ((q_ref, qp4, qp16), (k_ref, kp4, kp16), (v_ref, vp4, vp16)):
            f32buf[...] = src[0, :, cols].astype(F32)
            for r4 in range(4):
                x4 = f32buf[pl.ds(r4, l4, stride=4), :]
                f32buf4[r4 * l4:(r4 + 1) * l4, :] = x4
                p4[r4 * l4:(r4 + 1) * l4, :] = x4.astype(BF16)
            for r in range(16):
                a, r4 = divmod(r, 4)
                p16[r * l16:(r + 1) * l16, :] = f32buf4[pl.ds(r4 * l4 + a, l16, stride=4), :].astype(BF16)


        def body1(g, c):
            rows = pl.ds(pl.multiple_of(g * grows, grows), grows)
            starts = [pl.multiple_of(jnp.clip((g * GROUP + j) * QBLK - BAND, 0, seq - KWIN), BAND)
                      for j in range(GROUP)]
            k3 = jnp.stack([k_ref[0, pl.ds(ks, KWIN), cols] for ks in starts])
            v3 = jnp.stack([v_ref[0, pl.ds(ks, KWIN), cols] for ks in starts])
            o, lse = _softmax_blocks(blocks(q_ref[0, rows, cols]), k3, v3, bias1_ref[g])
            put(ob1, ls1, rows, o.reshape(grows, HEAD_DIM), lse.reshape(grows, 1))
            return c
        lax.fori_loop(0, ng1, body1, 0)

        def body4(g, c):
            base = pl.multiple_of(g * grows, grows)
            rows = pl.ds(base, grows)
            starts = [(j // nb4) * l4 + min(max((j % nb4) * QBLK - BAND, 0), l4 - KWIN) for j in range(GROUP)]
            k3 = jnp.stack([kp4[pl.ds(base + ks, KWIN), :] for ks in starts])
            v3 = jnp.stack([vp4[pl.ds(base + ks, KWIN), :] for ks in starts])
            o, lse = _softmax_blocks(blocks(qp4[rows, :]), k3, v3, bias4_ref[...])
            put(ob4, ls4, rows, o.reshape(grows, HEAD_DIM), lse.reshape(grows, 1))
            return c
        lax.fori_loop(0, seq // grows, body4, 0)

        def body16(g, c):
            rows = pl.ds(pl.multiple_of(g * grows, grows), grows)
            o, lse = _softmax_blocks(blocks(qp16[rows, :]), blocks(kp16[rows, :]), blocks(vp16[rows, :]),
                                     bias16_ref[...])
            for j in range(GROUP):
                a = g * (GROUP // 4) + j // 4
                put(ob16, ls16, pl.ds((j % 4) * l4 + a, l16, stride=4), o[j], lse[j])
            return c
        lax.fori_loop(0, 16 // GROUP, body16, 0)

        rc = 64
        for r4 in range(4):
            def comb(t, c, r4=r4):
                rows = pl.ds(pl.multiple_of(r4 * l4 + t * rc, rc), rc)
                nat = pl.ds(4 * t * rc + r4, rc, stride=4)
                a1, a4, a16 = ls1[nat, :], ls4[rows, :], ls16[rows, :]
                mx = jnp.maximum(jnp.maximum(a1, a4), a16)
                e1, e4, e16 = jnp.exp(a1 - mx), jnp.exp(a4 - mx), jnp.exp(a16 - mx)
                out = (e1 * ob1[nat, :] + e4 * ob4[rows, :] + e16 * ob16[rows, :]) / (e1 + e4 + e16)
                f32buf[nat, :] = out
                return c
            lax.fori_loop(0, l4 // rc, comb, 0, unroll=2)
        o_ref[0, :, cols] = f32buf[...].astype(BF16)


def _band_biases(seq):
    r = np.arange(QBLK)[:, None]
    c = np.arange(KWIN)[None, :]
    first = np.abs(c - r) <= BAND
    mid = np.abs(c - BAND - r) <= BAND
    last = np.abs(c - (KWIN - QBLK) - r) <= BAND
    neg = lambda valid: np.where(valid, 0.0, NEG).astype(np.float32)
    nb1 = seq // QBLK
    b1 = neg(np.stack([first] + [mid] * (nb1 - 2) + [last])).reshape(nb1 // GROUP, GROUP, QBLK, KWIN)
    nb4 = seq // (4 * QBLK)
    b4 = neg(np.stack(([first] + [mid] * (nb4 - 2) + [last]) * (GROUP // nb4)))
    c16 = np.arange(seq // 16)[None, :]
    r16 = np.arange(seq // 16)[:, None]
    b16 = neg(np.abs(c16 - r16) <= BAND)
    return jnp.asarray(b1), jnp.asarray(b4), jnp.asarray(b16)


def _attention(proj3, *, attn_w, heads=2):
    B, S, _ = proj3.shape
    assert S // 16 == QBLK and GROUP % (S // (4 * QBLK)) == 0 and S % (GROUP * QBLK) == 0 and GROUP % 4 == 0
    n_hg = attn_w // (heads * HEAD_DIM)
    hw = heads * HEAD_DIM
    b1, b4, b16 = _band_biases(S)
    kern = functools.partial(_attn_kernel, seq=S, heads=heads)
    perm = lambda: pltpu.VMEM((S, HEAD_DIM), BF16)
    nat = lambda: pltpu.VMEM((S, HEAD_DIM), F32)
    return pl.pallas_call(
        kern,
        grid=(B, n_hg),
        in_specs=[
            pl.BlockSpec((1, S, hw), lambda b, g: (b, 0, g)),
            pl.BlockSpec((1, S, hw), lambda b, g: (b, 0, n_hg + g)),
            pl.BlockSpec((1, S, hw), lambda b, g: (b, 0, 2 * n_hg + g)),
            pl.BlockSpec((S // (GROUP * QBLK), GROUP, QBLK, KWIN), lambda b, g: (0, 0, 0, 0)),
            pl.BlockSpec((GROUP, QBLK, KWIN), lambda b, g: (0, 0, 0)),
            pl.BlockSpec((S // 16, S // 16), lambda b, g: (0, 0)),
        ],
        out_specs=pl.BlockSpec((1, S, hw), lambda b, g: (b, 0, g)),
        out_shape=jax.ShapeDtypeStruct((B, S, attn_w), BF16),
        scratch_shapes=[nat(), nat()] + [perm() for _ in range(6)] + [nat() for _ in range(6)],
        compiler_params=_params(("parallel", "parallel")),
        name="dilated_attention",
    )(proj3, proj3, proj3, b1, b4, b16)


def _conv_kernel(prev_ref, main_ref, next_ref, w_ref, b_ref, lg_ref, lb_ref, o_ref, ubuf, ybuf,
                 *, tiles_per_seq, row_blk, norm_rows):
    tt, C = main_ref.shape
    n_cb, _, col_blk = ybuf.shape
    pos = pl.program_id(0) % tiles_per_seq
    zero_halo = jnp.zeros((CONV_HALO, col_blk), F32)
    for cb in range(n_cb):
        cols = slice(cb * col_blk, (cb + 1) * col_blk)
        ubuf[cb, 0:CONV_HALO, :] = jnp.where(pos != 0, prev_ref[:, cols].astype(F32), zero_halo)
        ubuf[cb, CONV_HALO:CONV_HALO + tt, :] = main_ref[:, cols].astype(F32)
        ubuf[cb, CONV_HALO + tt:, :] = jnp.where(pos != tiles_per_seq - 1, next_ref[:, cols].astype(F32),
                                                 zero_halo)

    def conv_body(cb, c):
        for rb in range(tt // (2 * row_blk)):
            for parity in range(2):
                first = rb * 2 * row_blk + parity
                acc = jnp.zeros((row_blk, col_blk), F32)
                for k in range(CONV_TAPS):
                    r = first + CONV_HALO - CONV_PAD + k
                    acc = acc + ubuf[cb, pl.ds(r, row_blk, stride=2), :] * w_ref[cb, k:k + 1, :]
                ybuf[cb, pl.ds(first, row_blk, stride=2), :] = acc + b_ref[cb]
        return c
    lax.fori_loop(0, n_cb, conv_body, 0)

    def norm_body(rb, c):
        rows = pl.ds(pl.multiple_of(rb * norm_rows, norm_rows), norm_rows)
        ys = [ybuf[cb, rows, :] for cb in range(n_cb)]
        mu = jnp.sum(sum(ys), axis=-1, keepdims=True) * (1.0 / C)
        ycs = [y - mu for y in ys]
        var = jnp.sum(sum(yc * yc for yc in ycs), axis=-1, keepdims=True) * (1.0 / C)
        rstd = lax.rsqrt(var + EPS)
        for cb in range(n_cb):
            cols = slice(cb * col_blk, (cb + 1) * col_blk)
            z = ycs[cb] * rstd * lg_ref[:, cols] + lb_ref[:, cols]
            o_ref[rows, cols] = (z * jax.nn.sigmoid(z)).astype(BF16)
        return c
    lax.fori_loop(0, tt // norm_rows, norm_body, 0, unroll=2)


def _conformer_conv(proj2, conv_w, conv_b, ln_g, ln_b, *, seq, conv_width, tt=256, col_blk=LANES):
    T, W = proj2.shape
    C = conv_width
    n_cb = C // col_blk
    ucol = (W - C) // C
    hb = tt // CONV_HALO
    n_halo = T // CONV_HALO
    w_blk = conv_w.reshape(CONV_TAPS, n_cb, col_blk).transpose(1, 0, 2)
    b_blk = conv_b.reshape(n_cb, 1, col_blk)
    kern = functools.partial(_conv_kernel, tiles_per_seq=seq // tt, row_blk=tt // 2, norm_rows=32)
    vec = lambda: pl.BlockSpec((1, C), lambda i: (0, 0))
    return pl.pallas_call(
        kern,
        grid=(T // tt,),
        in_specs=[
            pl.BlockSpec((CONV_HALO, C), lambda i: (jnp.maximum(i * hb - 1, 0), ucol)),
            pl.BlockSpec((tt, C), lambda i: (i, ucol)),
            pl.BlockSpec((CONV_HALO, C), lambda i: (jnp.minimum((i + 1) * hb, n_halo - 1), ucol)),
            pl.BlockSpec((n_cb, CONV_TAPS, col_blk), lambda i: (0, 0, 0)),
            pl.BlockSpec((n_cb, 1, col_blk), lambda i: (0, 0, 0)),
            vec(), vec(),
        ],
        out_specs=pl.BlockSpec((tt, C), lambda i: (i, 0)),
        out_shape=jax.ShapeDtypeStruct((T, C), BF16),
        scratch_shapes=[pltpu.VMEM((n_cb, tt + 2 * CONV_HALO, col_blk), F32),
                        pltpu.VMEM((n_cb, tt, col_blk), F32)],
        compiler_params=_params(("parallel",)),
        name="conformer_conv",
    )(proj2, proj2, proj2, w_blk, b_blk, ln_g, ln_b)


def _outproj_kernel(attn_ref, conv_ref, w_ref, x_ref, g_ref, h_ref, hb_ref, ssq_ref):
    j = pl.program_id(1)
    A = attn_ref.shape[1]
    acc = jnp.dot(attn_ref[...], w_ref[0:A, :], preferred_element_type=F32)
    acc = acc + jnp.dot(conv_ref[...], w_ref[A:, :], preferred_element_type=F32)
    h = x_ref[...] + acc
    h_ref[...] = h
    hb_ref[...] = (h * g_ref[...]).astype(BF16)
    hh = h * h
    part = hh[:, 0:LANES]
    for c in range(1, h.shape[1] // LANES):
        part = part + hh[:, c * LANES:(c + 1) * LANES]

    @pl.when(j == 0)
    def _():
        ssq_ref[...] = part

    @pl.when(j != 0)
    def _():
        ssq_ref[...] += part


def _out_projection(attn2, conv2, w_out_bf, x2, ffn_g, *, tm=1024, tn=W_TILE):
    T, D = x2.shape
    A = attn2.shape[1]
    C = conv2.shape[1]
    return pl.pallas_call(
        _outproj_kernel,
        grid=(T // tm, D // tn),
        in_specs=[
            pl.BlockSpec((tm, A), lambda i, j: (i, 0)),
            pl.BlockSpec((tm, C), lambda i, j: (i, 0)),
            pl.BlockSpec((A + C, tn), lambda i, j: (0, j)),
            pl.BlockSpec((tm, tn), lambda i, j: (i, j)),
            pl.BlockSpec((1, tn), lambda i, j: (0, j)),
        ],
        out_specs=[
            pl.BlockSpec((tm, tn), lambda i, j: (i, j)),
            pl.BlockSpec((tm, tn), lambda i, j: (i, j)),
            pl.BlockSpec((tm, LANES), lambda i, j: (i, 0)),
        ],
        out_shape=[
            jax.ShapeDtypeStruct((T, D), F32),
            jax.ShapeDtypeStruct((T, D), BF16),
            jax.ShapeDtypeStruct((T, LANES), F32),
        ],
        compiler_params=_params(("parallel", "arbitrary")),
        name="out_projection",
    )(attn2, conv2, w_out_bf, x2, ffn_g)


def _ffn_kernel(hb_ref, ssq_ref, h_hbm, wg_hbm, wu_hbm, wd_hbm, o_ref, r_ref, wg_buf, wu_buf, wd_buf,
                sem_res, sem_w, *, d_model, d_ff):
    i = pl.program_id(0)
    ni = pl.num_programs(0)
    tm = o_ref.shape[0]
    tf = wg_buf.shape[2]
    n_full = d_ff // tf
    tail = d_ff - n_full * tf

    def residual_copy():
        return pltpu.make_async_copy(h_hbm.at[pl.ds(pl.multiple_of(i * tm, tm), tm), :], o_ref, sem_res)

    def tile_copies(t, slot, width):
        start = t * tf if isinstance(t, int) else pl.multiple_of(t * tf, tf)
        src = pl.ds(start, width)
        dst = pl.ds(0, width)
        return (
            pltpu.make_async_copy(wg_hbm.at[:, src], wg_buf.at[slot, :, dst], sem_w.at[0, slot]),
            pltpu.make_async_copy(wu_hbm.at[:, src], wu_buf.at[slot, :, dst], sem_w.at[1, slot]),
            pltpu.make_async_copy(wd_hbm.at[src, :], wd_buf.at[slot, dst, :], sem_w.at[2, slot]),
        )

    def start_tile(t, slot, width):
        for c in tile_copies(t, slot, width):
            c.start()

    def wait_tile(t, slot, width):
        for c in tile_copies(t, slot, width):
            c.wait()

    def multiply(slot, width, before_accumulate=None):
        hb = hb_ref[...]
        r = r_ref[:, :width]
        g = jnp.dot(hb, wg_buf[slot, :, :width], preferred_element_type=F32) * r
        u = jnp.dot(hb, wu_buf[slot, :, :width], preferred_element_type=F32) * r
        a = (g * jax.nn.sigmoid(g) * u).astype(BF16)
        if before_accumulate is not None:
            before_accumulate()
        o_ref[...] += jnp.dot(a, wd_buf[slot, :width, :], preferred_element_type=F32)

    residual_copy().start()

    @pl.when(i == 0)
    def _():
        start_tile(0, 0, tf)

    ms = jnp.sum(ssq_ref[...], axis=-1, keepdims=True) * (1.0 / d_model)
    r_ref[...] = jnp.broadcast_to(lax.rsqrt(ms + EPS), r_ref.shape)

    def pair(p, c):
        t = 2 * p
        wait_tile(t, 0, tf)
        start_tile(t + 1, 1, tf)

        def residual_landed():
            @pl.when(p == 0)
            def _():
                residual_copy().wait()
        multiply(0, tf, residual_landed)
        wait_tile(t + 1, 1, tf)
        start_tile(t + 2, 0, tf)
        multiply(1, tf)
        return c
    lax.fori_loop(0, (n_full - 1) // 2, pair, 0)

    wait_tile(n_full - 1, 0, tf)
    start_tile(n_full, 1, tail)
    multiply(0, tf)
    wait_tile(n_full, 1, tail)

    @pl.when(i + 1 < ni)
    def _():
        start_tile(0, 0, tf)
    multiply(1, tail)


def _ffn(hb, ssq, h, wg_bf, wu_bf, wd_bf, *, tm=512, tf=W_TILE):
    T, D = h.shape
    d_ff = wg_bf.shape[1]
    assert (d_ff // tf) % 2 == 1 and d_ff // tf >= 3 and d_ff % tf != 0 and d_ff % LANES == 0
    kern = functools.partial(_ffn_kernel, d_model=D, d_ff=d_ff)
    hbm = pl.BlockSpec(memory_space=pl.ANY)
    return pl.pallas_call(
        kern,
        grid=(T // tm,),
        in_specs=[
            pl.BlockSpec((tm, D), lambda i: (i, 0)),
            pl.BlockSpec((tm, LANES), lambda i: (i, 0)),
            hbm, hbm, hbm, hbm,
        ],
        out_specs=pl.BlockSpec((tm, D), lambda i: (i, 0)),
        out_shape=jax.ShapeDtypeStruct((T, D), F32),
        scratch_shapes=[
            pltpu.VMEM((tm, tf), F32),
            pltpu.VMEM((2, D, tf), BF16), pltpu.VMEM((2, D, tf), BF16), pltpu.VMEM((2, tf, D), BF16),
            pltpu.SemaphoreType.DMA(()), pltpu.SemaphoreType.DMA((3, 2)),
        ],
        compiler_params=_params(("arbitrary",), FFN_VMEM_LIMIT_BYTES),
        name="swiglu_ffn",
    )(hb, ssq, h, wg_bf, wu_bf, wd_bf)


def _rope_tables(seq):
    half = HEAD_DIM // 2
    freqs = ROPE_THETA ** (-jnp.arange(half, dtype=F32) * 2.0 / HEAD_DIM)
    ang = jnp.arange(seq, dtype=F32)[:, None] * freqs[None, :]
    cos = jnp.cos(ang)
    sin = jnp.sin(ang)
    return jnp.concatenate([cos, cos], axis=-1), jnp.concatenate([-sin, sin], axis=-1)


def _layer(x, p):
    B, S, D = x.shape
    T = B * S
    attn_w = D // 2
    conv_w = D - attn_w
    x2 = x.reshape(T, D)
    qkv, u = _in_projection(x2, p["norm_mix_g"], p["w_in"], p["cos"], p["sin"], p["q_norm_g"], p["k_norm_g"],
                            seq=S, attn_w=attn_w, conv_w=conv_w)
    attn = _attention(qkv.reshape(B, S, -1), attn_w=attn_w)
    conv = _conformer_conv(u, p["conv_w"], p["conv_b"], p["conv_ln_g"], p["conv_ln_b"],
                           seq=S, conv_width=conv_w)
    h, hb, ssq = _out_projection(attn.reshape(T, attn_w), conv, p["w_out"], x2, p["norm_ffn_g"])
    y = _ffn(hb, ssq, h, p["w_gate"], p["w_up"], p["w_down"])
    return y.reshape(B, S, D)


def kernel(x_prompt, x_sample, norm_mix_g, w_in, q_norm_g, k_norm_g, conv_w, conv_b, conv_ln_g, conv_ln_b,
           w_out, norm_ffn_g, w_gate, w_up, w_down):
    depth = w_in.shape[0]
    y_prompt, y_sample = x_prompt, x_sample
    cos, sin = _rope_tables(x_prompt.shape[1])
    assert x_sample.shape[1] == x_prompt.shape[1]
    row = lambda v: v.reshape(1, -1)
    for l in range(depth):
        p = dict(
            norm_mix_g=row(norm_mix_g[l]), w_in=w_in[l].astype(BF16), cos=cos, sin=sin,
            q_norm_g=row(q_norm_g[l]), k_norm_g=row(k_norm_g[l]),
            conv_w=conv_w[l], conv_b=row(conv_b[l]), conv_ln_g=row(conv_ln_g[l]), conv_ln_b=row(conv_ln_b[l]),
            w_out=w_out[l].astype(BF16), norm_ffn_g=row(norm_ffn_g[l]),
            w_gate=w_gate[l].astype(BF16), w_up=w_up[l].astype(BF16), w_down=w_down[l].astype(BF16),
        )
        y_prompt = _layer(y_prompt, p)
        y_sample = _layer(y_sample, p)
    return (y_prompt, y_sample)
```

```python
import functools

import numpy as np
import jax
import jax.numpy as jnp
from jax import lax
from jax.experimental import pallas as pl
from jax.experimental.pallas import tpu as pltpu

F32 = jnp.float32
BF16 = jnp.bfloat16

HEAD_DIM = 128
LANES = 128
CONV_TAPS = 31
CONV_PAD = (CONV_TAPS - 1) // 2
CONV_HALO = 16
DILATIONS = (1, 4, 16)
BAND = 64
QBLK = 128
KWIN = QBLK + 2 * BAND
GROUP = 16
ROPE_THETA = 10000.0
EPS = 1e-6
NEG = -1e30
VMEM_LIMIT_BYTES = 56 * 1024 * 1024
FFN_VMEM_LIMIT_BYTES = 60 * 1024 * 1024
W_TILE = 512


def _params(semantics, vmem_limit_bytes=VMEM_LIMIT_BYTES):
    return pltpu.CompilerParams(dimension_semantics=semantics, vmem_limit_bytes=vmem_limit_bytes)


def _inproj_kernel(x_ref, g_ref, wa_ref, wb_ref, cos_ref, sin_ref, qg_ref, kg_ref, qkv_ref, u_ref, hn0_ref, hn1_ref,
                   *, n_q, n_qkv, scale, norm_rows, n_chunks):
    i = pl.program_id(0)
    j = pl.program_id(1)
    tm, tn = u_ref.shape
    chunk_rows = tm // n_chunks

    def normalise(hn_ref, first_row, n_rows):
        for t in range(n_rows // norm_rows):
            rows = pl.ds(pl.multiple_of(first_row + t * norm_rows, norm_rows), norm_rows)
            x = x_ref[rows, :]
            ms = jnp.mean(x * x, axis=-1, keepdims=True)
            hn_ref[rows, :] = (x * lax.rsqrt(ms + EPS) * g_ref[...]).astype(BF16)

    @pl.when((i == 0) & (j == 0))
    def _():
        def body(t, c):
            normalise(hn0_ref, t * chunk_rows, chunk_rows)
            return c
        lax.fori_loop(0, n_chunks, body, 0)

    def steps(hn_ref, hn_next_ref):
        def prepare_next_tile():
            chunk = jnp.clip(j - 1, 0, n_chunks - 1)
            normalise(hn_next_ref, chunk * chunk_rows, chunk_rows)

        def proj(t):
            w_ref = (wa_ref, wb_ref)[t]
            return jnp.dot(hn_ref[...], w_ref[...], preferred_element_type=F32)

        @pl.when(j < 2 * n_q)
        def _():
            prepare_next_tile()
            gain = jnp.where(j < n_q, qg_ref[...] * scale, kg_ref[...])
            cos = cos_ref[...]
            sin = sin_ref[...]
            for t in range(2):
                acc = proj(t)
                heads = [acc[:, h * HEAD_DIM:(h + 1) * HEAD_DIM] for h in range(tn // HEAD_DIM)]
                ms = [jnp.mean(xh * xh, axis=-1, keepdims=True) for xh in heads]
                ys = [xh * lax.rsqrt(m + EPS) * gain for xh, m in zip(heads, ms)]
                rot = [pltpu.roll(y, HEAD_DIM // 2, 1) for y in ys]
                for h, (y, yr) in enumerate(zip(ys, rot)):
                    col = t * tn + h * HEAD_DIM
                    qkv_ref[:, col:col + HEAD_DIM] = (y * cos + yr * sin).astype(BF16)

        @pl.when((j >= 2 * n_q) & (j < n_qkv))
        def _():
            prepare_next_tile()
            for t in range(2):
                qkv_ref[:, t * tn:(t + 1) * tn] = proj(t).astype(BF16)

        @pl.when(j >= n_qkv)
        def _():
            prepare_next_tile()
            u_ref[...] = (proj(0) * jax.nn.sigmoid(proj(1))).astype(BF16)

    @pl.when(lax.rem(i, 2) == 0)
    def _():
        steps(hn0_ref, hn1_ref)

    @pl.when(lax.rem(i, 2) == 1)
    def _():
        steps(hn1_ref, hn0_ref)


def _in_projection(x2, norm_g, w_in_bf, cos_t, sin_t, qg, kg, *, seq, attn_w, conv_w, tm=512, tn=W_TILE):
    T, D = x2.shape
    n_q = attn_w // (2 * tn)
    n_qkv = 3 * n_q
    n_c = conv_w // tn
    n_steps = n_qkv + n_c
    n_i = T // tm
    kern = functools.partial(_inproj_kernel, n_q=n_q, n_qkv=n_qkv, scale=HEAD_DIM ** -0.5, norm_rows=16,
                             n_chunks=8)
    return pl.pallas_call(
        kern,
        grid=(n_i, n_steps),
        in_specs=[
            pl.BlockSpec((tm, D), lambda i, j: (jnp.minimum(i + jnp.minimum(j, 1), n_i - 1), 0)),
            pl.BlockSpec((1, D), lambda i, j: (0, 0)),
            pl.BlockSpec((D, tn), lambda i, j: (0, jnp.where(j < n_qkv, 2 * j, j + n_qkv))),
            pl.BlockSpec((D, tn), lambda i, j: (0, jnp.where(j < n_qkv, 2 * j + 1, j + n_qkv + n_c))),
            pl.BlockSpec((tm, HEAD_DIM), lambda i, j: (i % (seq // tm), 0)),
            pl.BlockSpec((tm, HEAD_DIM), lambda i, j: (i % (seq // tm), 0)),
            pl.BlockSpec((1, HEAD_DIM), lambda i, j: (0, 0)),
            pl.BlockSpec((1, HEAD_DIM), lambda i, j: (0, 0)),
        ],
        out_specs=[
            pl.BlockSpec((tm, 2 * tn), lambda i, j: (i, jnp.minimum(j, n_qkv - 1))),
            pl.BlockSpec((tm, tn), lambda i, j: (i, jnp.maximum(j - n_qkv, 0))),
        ],
        out_shape=[
            jax.ShapeDtypeStruct((T, 3 * attn_w), BF16),
            jax.ShapeDtypeStruct((T, conv_w), BF16),
        ],
        scratch_shapes=[pltpu.VMEM((tm, D), BF16), pltpu.VMEM((tm, D), BF16)],
        compiler_params=_params(("arbitrary", "arbitrary")),
        name="in_projection",
    )(x2, norm_g, w_in_bf, w_in_bf, cos_t, sin_t, qg, kg)


def _softmax_blocks(q, k, v, bias):
    s = jnp.einsum("gqd,gkd->gqk", q, k, preferred_element_type=F32) + bias
    m = jnp.max(s, axis=-1, keepdims=True)
    p = jnp.exp(s - m)
    l = jnp.sum(p, axis=-1, keepdims=True)
    o = jnp.einsum("gqk,gkd->gqd", p.astype(BF16), v, preferred_element_type=F32) * (1.0 / l)
    return o, m + jnp.log(l)


def _attn_kernel(q_ref, k_ref, v_ref, bias1_ref, bias4_ref, bias16_ref, o_ref,
                 f32buf, f32buf4, qp4, kp4, vp4, qp16, kp16, vp16, ob1, ob4, ob16, ls1, ls4, ls16,
                 *, seq, heads):
    l4 = seq // 4
    l16 = seq // 16
    nb4 = l4 // QBLK
    ng1 = seq // (GROUP * QBLK)
    grows = GROUP * QBLK
    blocks = lambda x: x.reshape(GROUP, QBLK, HEAD_DIM)

    def put(ob, ls, rows, o, lse):
        ob[rows, :] = o
        ls[rows, :] = jnp.broadcast_to(lse, o.shape)

    for h in range(heads):
        cols = slice(h * HEAD_DIM, (h + 1) * HEAD_DIM)

        for src, p4, p16 in ((q_ref, qp4, qp16), (k_ref, kp4, kp16), (v_ref, vp4, vp16)):
            f32buf[...] = src[0, :, cols].astype(F32)
            for r4 in range(4):
                x4 = f32buf[pl.ds(r4, l4, stride=4), :]
                f32buf4[r4 * l4:(r4 + 1) * l4, :] = x4
                p4[r4 * l4:(r4 + 1) * l4, :] = x4.astype(BF16)
            for r in range(16):
                a, r4 = divmod(r, 4)
                p16[r * l16:(r + 1) * l16, :] = f32buf4[pl.ds(r4 * l4 + a, l16, stride=4), :].astype(BF16)


        def body1(g, c):
            rows = pl.ds(pl.multiple_of(g * grows, grows), grows)
            starts = [pl.multiple_of(jnp.clip((g * GROUP + j) * QBLK - BAND, 0, seq - KWIN), BAND)
                      for j in range(GROUP)]
            k3 = jnp.stack([k_ref[0, pl.ds(ks, KWIN), cols] for ks in starts])
            v3 = jnp.stack([v_ref[0, pl.ds(ks, KWIN), cols] for ks in starts])
            o, lse = _softmax_blocks(blocks(q_ref[0, rows, cols]), k3, v3, bias1_ref[g])
            put(ob1, ls1, rows, o.reshape(grows, HEAD_DIM), lse.reshape(grows, 1))
            return c
        lax.fori_loop(0, ng1, body1, 0)

        def body4(g, c):
            base = pl.multiple_of(g * grows, grows)
            rows = pl.ds(base, grows)
            starts = [(j // nb4) * l4 + min(max((j % nb4) * QBLK - BAND, 0), l4 - KWIN) for j in range(GROUP)]
            k3 = jnp.stack([kp4[pl.ds(base + ks, KWIN), :] for ks in starts])
            v3 = jnp.stack([vp4[pl.ds(base + ks, KWIN), :] for ks in starts])
            o, lse = _softmax_blocks(blocks(qp4[rows, :]), k3, v3, bias4_ref[...])
            put(ob4, ls4, rows, o.reshape(grows, HEAD_DIM), lse.reshape(grows, 1))
            return c
        lax.fori_loop(0, seq // grows, body4, 0)

        def body16(g, c):
            rows = pl.ds(pl.multiple_of(g * grows, grows), grows)
            o, lse = _softmax_blocks(blocks(qp16[rows, :]), blocks(kp16[rows, :]), blocks(vp16[rows, :]),
                                     bias16_ref[...])
            for j in range(GROUP):
                a = g * (GROUP // 4) + j // 4
                put(ob16, ls16, pl.ds((j % 4) * l4 + a, l16, stride=4), o[j], lse[j])
            return c
        lax.fori_loop(0, 16 // GROUP, body16, 0)

        rc = 64
        for r4 in range(4):
            def comb(t, c, r4=r4):
                rows = pl.ds(pl.multiple_of(r4 * l4 + t * rc, rc), rc)
                nat = pl.ds(4 * t * rc + r4, rc, stride=4)
                a1, a4, a16 = ls1[nat, :], ls4[rows, :], ls16[rows, :]
                mx = jnp.maximum(jnp.maximum(a1, a4), a16)
                e1, e4, e16 = jnp.exp(a1 - mx), jnp.exp(a4 - mx), jnp.exp(a16 - mx)
                out = (e1 * ob1[nat, :] + e4 * ob4[rows, :] + e16 * ob16[rows, :]) / (e1 + e4 + e16)
                f32buf[nat, :] = out
                return c
            lax.fori_loop(0, l4 // rc, comb, 0, unroll=2)
        o_ref[0, :, cols] = f32buf[...].astype(BF16)


def _band_biases(seq):
    r = np.arange(QBLK)[:, None]
    c = np.arange(KWIN)[None, :]
    first = np.abs(c - r) <= BAND
    mid = np.abs(c - BAND - r) <= BAND
    last = np.abs(c - (KWIN - QBLK) - r) <= BAND
    neg = lambda valid: np.where(valid, 0.0, NEG).astype(np.float32)
    nb1 = seq // QBLK
    b1 = neg(np.stack([first] + [mid] * (nb1 - 2) + [last])).reshape(nb1 // GROUP, GROUP, QBLK, KWIN)
    nb4 = seq // (4 * QBLK)
    b4 = neg(np.stack(([first] + [mid] * (nb4 - 2) + [last]) * (GROUP // nb4)))
    c16 = np.arange(seq // 16)[None, :]
    r16 = np.arange(seq // 16)[:, None]
    b16 = neg(np.abs(c16 - r16) <= BAND)
    return jnp.asarray(b1), jnp.asarray(b4), jnp.asarray(b16)


def _attention(proj3, *, attn_w, heads=2):
    B, S, _ = proj3.shape
    assert S // 16 == QBLK and GROUP % (S // (4 * QBLK)) == 0 and S % (GROUP * QBLK) == 0 and GROUP % 4 == 0
    n_hg = attn_w // (heads * HEAD_DIM)
    hw = heads * HEAD_DIM
    b1, b4, b16 = _band_biases(S)
    kern = functools.partial(_attn_kernel, seq=S, heads=heads)
    perm = lambda: pltpu.VMEM((S, HEAD_DIM), BF16)
    nat = lambda: pltpu.VMEM((S, HEAD_DIM), F32)
    return pl.pallas_call(
        kern,
        grid=(B, n_hg),
        in_specs=[
            pl.BlockSpec((1, S, hw), lambda b, g: (b, 0, g)),
            pl.BlockSpec((1, S, hw), lambda b, g: (b, 0, n_hg + g)),
            pl.BlockSpec((1, S, hw), lambda b, g: (b, 0, 2 * n_hg + g)),
            pl.BlockSpec((S // (GROUP * QBLK), GROUP, QBLK, KWIN), lambda b, g: (0, 0, 0, 0)),
            pl.BlockSpec((GROUP, QBLK, KWIN), lambda b, g: (0, 0, 0)),
            pl.BlockSpec((S // 16, S // 16), lambda b, g: (0, 0)),
        ],
        out_specs=pl.BlockSpec((1, S, hw), lambda b, g: (b, 0, g)),
        out_shape=jax.ShapeDtypeStruct((B, S, attn_w), BF16),
        scratch_shapes=[nat(), nat()] + [perm() for _ in range(6)] + [nat() for _ in range(6)],
        compiler_params=_params(("parallel", "parallel")),
        name="dilated_attention",
    )(proj3, proj3, proj3, b1, b4, b16)


def _conv_kernel(prev_ref, main_ref, next_ref, w_ref, b_ref, lg_ref, lb_ref, o_ref, ubuf, ybuf,
                 *, tiles_per_seq, row_blk, norm_rows):
    tt, C = main_ref.shape
    n_cb, _, col_blk = ybuf.shape
    pos = pl.program_id(0) % tiles_per_seq
    zero_halo = jnp.zeros((CONV_HALO, col_blk), F32)
    for cb in range(n_cb):
        cols = slice(cb * col_blk, (cb + 1) * col_blk)
        ubuf[cb, 0:CONV_HALO, :] = jnp.where(pos != 0, prev_ref[:, cols].astype(F32), zero_halo)
        ubuf[cb, CONV_HALO:CONV_HALO + tt, :] = main_ref[:, cols].astype(F32)
        ubuf[cb, CONV_HALO + tt:, :] = jnp.where(pos != tiles_per_seq - 1, next_ref[:, cols].astype(F32),
                                                 zero_halo)

    def conv_body(cb, c):
        for rb in range(tt // (2 * row_blk)):
            for parity in range(2):
                first = rb * 2 * row_blk + parity
                acc = jnp.zeros((row_blk, col_blk), F32)
                for k in range(CONV_TAPS):
                    r = first + CONV_HALO - CONV_PAD + k
                    acc = acc + ubuf[cb, pl.ds(r, row_blk, stride=2), :] * w_ref[cb, k:k + 1, :]
                ybuf[cb, pl.ds(first, row_blk, stride=2), :] = acc + b_ref[cb]
        return c
    lax.fori_loop(0, n_cb, conv_body, 0)

    def norm_body(rb, c):
        rows = pl.ds(pl.multiple_of(rb * norm_rows, norm_rows), norm_rows)
        ys = [ybuf[cb, rows, :] for cb in range(n_cb)]
        mu = jnp.sum(sum(ys), axis=-1, keepdims=True) * (1.0 / C)
        ycs = [y - mu for y in ys]
        var = jnp.sum(sum(yc * yc for yc in ycs), axis=-1, keepdims=True) * (1.0 / C)
        rstd = lax.rsqrt(var + EPS)
        for cb in range(n_cb):
            cols = slice(cb * col_blk, (cb + 1) * col_blk)
            z = ycs[cb] * rstd * lg_ref[:, cols] + lb_ref[:, cols]
            o_ref[rows, cols] = (z * jax.nn.sigmoid(z)).astype(BF16)
        return c
    lax.fori_loop(0, tt // norm_rows, norm_body, 0, unroll=2)


def _conformer_conv(proj2, conv_w, conv_b, ln_g, ln_b, *, seq, conv_width, tt=256, col_blk=LANES):
    T, W = proj2.shape
    C = conv_width
    n_cb = C // col_blk
    ucol = (W - C) // C
    hb = tt // CONV_HALO
    n_halo = T // CONV_HALO
    w_blk = conv_w.reshape(CONV_TAPS, n_cb, col_blk).transpose(1, 0, 2)
    b_blk = conv_b.reshape(n_cb, 1, col_blk)
    kern = functools.partial(_conv_kernel, tiles_per_seq=seq // tt, row_blk=tt // 2, norm_rows=64)
    vec = lambda: pl.BlockSpec((1, C), lambda i: (0, 0))
    return pl.pallas_call(
        kern,
        grid=(T // tt,),
        in_specs=[
            pl.BlockSpec((CONV_HALO, C), lambda i: (jnp.maximum(i * hb - 1, 0), ucol)),
            pl.BlockSpec((tt, C), lambda i: (i, ucol)),
            pl.BlockSpec((CONV_HALO, C), lambda i: (jnp.minimum((i + 1) * hb, n_halo - 1), ucol)),
            pl.BlockSpec((n_cb, CONV_TAPS, col_blk), lambda i: (0, 0, 0)),
            pl.BlockSpec((n_cb, 1, col_blk), lambda i: (0, 0, 0)),
            vec(), vec(),
        ],
        out_specs=pl.BlockSpec((tt, C), lambda i: (i, 0)),
        out_shape=jax.ShapeDtypeStruct((T, C), BF16),
        scratch_shapes=[pltpu.VMEM((n_cb, tt + 2 * CONV_HALO, col_blk), F32),
                        pltpu.VMEM((n_cb, tt, col_blk), F32)],
        compiler_params=_params(("parallel",)),
        name="conformer_conv",
    )(proj2, proj2, proj2, w_blk, b_blk, ln_g, ln_b)


def _outproj_kernel(attn_ref, conv_ref, w_ref, x_ref, g_ref, h_ref, hb_ref, ssq_ref):
    j = pl.program_id(1)
    A = attn_ref.shape[1]
    acc = jnp.dot(attn_ref[...], w_ref[0:A, :], preferred_element_type=F32)
    acc = acc + jnp.dot(conv_ref[...], w_ref[A:, :], preferred_element_type=F32)
    h = x_ref[...] + acc
    h_ref[...] = h
    hb_ref[...] = (h * g_ref[...]).astype(BF16)
    hh = h * h
    part = hh[:, 0:LANES]
    for c in range(1, h.shape[1] // LANES):
        part = part + hh[:, c * LANES:(c + 1) * LANES]

    @pl.when(j == 0)
    def _():
        ssq_ref[...] = part

    @pl.when(j != 0)
    def _():
        ssq_ref[...] += part


def _out_projection(attn2, conv2, w_out_bf, x2, ffn_g, *, tm=1024, tn=W_TILE):
    T, D = x2.shape
    A = attn2.shape[1]
    C = conv2.shape[1]
    return pl.pallas_call(
        _outproj_kernel,
        grid=(T // tm, D // tn),
        in_specs=[
            pl.BlockSpec((tm, A), lambda i, j: (i, 0)),
            pl.BlockSpec((tm, C), lambda i, j: (i, 0)),
            pl.BlockSpec((A + C, tn), lambda i, j: (0, j)),
            pl.BlockSpec((tm, tn), lambda i, j: (i, j)),
            pl.BlockSpec((1, tn), lambda i, j: (0, j)),
        ],
        out_specs=[
            pl.BlockSpec((tm, tn), lambda i, j: (i, j)),
            pl.BlockSpec((tm, tn), lambda i, j: (i, j)),
            pl.BlockSpec((tm, LANES), lambda i, j: (i, 0)),
        ],
        out_shape=[
            jax.ShapeDtypeStruct((T, D), F32),
            jax.ShapeDtypeStruct((T, D), BF16),
            jax.ShapeDtypeStruct((T, LANES), F32),
        ],
        compiler_params=_params(("parallel", "arbitrary")),
        name="out_projection",
    )(attn2, conv2, w_out_bf, x2, ffn_g)


def _ffn_kernel(hb_ref, ssq_ref, h_hbm, wg_hbm, wu_hbm, wd_hbm, o_ref, r_ref, wg_buf, wu_buf, wd_buf,
                sem_res, sem_w, *, d_model, d_ff):
    i = pl.program_id(0)
    ni = pl.num_programs(0)
    tm = o_ref.shape[0]
    tf = wg_buf.shape[2]
    n_full = d_ff // tf
    tail = d_ff - n_full * tf

    def residual_copy():
        return pltpu.make_async_copy(h_hbm.at[pl.ds(pl.multiple_of(i * tm, tm), tm), :], o_ref, sem_res)

    def tile_copies(t, slot, width):
        start = t * tf if isinstance(t, int) else pl.multiple_of(t * tf, tf)
        src = pl.ds(start, width)
        dst = pl.ds(0, width)
        return (
            pltpu.make_async_copy(wg_hbm.at[:, src], wg_buf.at[slot, :, dst], sem_w.at[0, slot]),
            pltpu.make_async_copy(wu_hbm.at[:, src], wu_buf.at[slot, :, dst], sem_w.at[1, slot]),
            pltpu.make_async_copy(wd_hbm.at[src, :], wd_buf.at[slot, dst, :], sem_w.at[2, slot]),
        )

    def start_tile(t, slot, width):
        for c in tile_copies(t, slot, width):
            c.start()

    def wait_tile(t, slot, width):
        for c in tile_copies(t, slot, width):
            c.wait()

    def multiply(slot, width, before_accumulate=None):
        hb = hb_ref[...]
        r = r_ref[:, :width]
        g = jnp.dot(hb, wg_buf[slot, :, :width], preferred_element_type=F32) * r
        u = jnp.dot(hb, wu_buf[slot, :, :width], preferred_element_type=F32) * r
        a = (g * jax.nn.sigmoid(g) * u).astype(BF16)
        if before_accumulate is not None:
            before_accumulate()
        o_ref[...] += jnp.dot(a, wd_buf[slot, :width, :], preferred_element_type=F32)

    residual_copy().start()

    @pl.when(i == 0)
    def _():
        start_tile(0, 0, tf)

    ms = jnp.sum(ssq_ref[...], axis=-1, keepdims=True) * (1.0 / d_model)
    r_ref[...] = jnp.broadcast_to(lax.rsqrt(ms + EPS), r_ref.shape)

    def pair(p, c):
        t = 2 * p
        wait_tile(t, 0, tf)
        start_tile(t + 1, 1, tf)

        def residual_landed():
            @pl.when(p == 0)
            def _():
                residual_copy().wait()
        multiply(0, tf, residual_landed)
        wait_tile(t + 1, 1, tf)
        start_tile(t + 2, 0, tf)
        multiply(1, tf)
        return c
    lax.fori_loop(0, (n_full - 1) // 2, pair, 0)

    wait_tile(n_full - 1, 0, tf)
    start_tile(n_full, 1, tail)
    multiply(0, tf)
    wait_tile(n_full, 1, tail)

    @pl.when(i + 1 < ni)
    def _():
        start_tile(0, 0, tf)
    multiply(1, tail)


def _ffn(hb, ssq, h, wg_bf, wu_bf, wd_bf, *, tm=512, tf=W_TILE):
    T, D = h.shape
    d_ff = wg_bf.shape[1]
    assert (d_ff // tf) % 2 == 1 and d_ff // tf >= 3 and d_ff % tf != 0 and d_ff % LANES == 0
    kern = functools.partial(_ffn_kernel, d_model=D, d_ff=d_ff)
    hbm = pl.BlockSpec(memory_space=pl.ANY)
    return pl.pallas_call(
        kern,
        grid=(T // tm,),
        in_specs=[
            pl.BlockSpec((tm, D), lambda i: (i, 0)),
            pl.BlockSpec((tm, LANES), lambda i: (i, 0)),
            hbm, hbm, hbm, hbm,
        ],
        out_specs=pl.BlockSpec((tm, D), lambda i: (i, 0)),
        out_shape=jax.ShapeDtypeStruct((T, D), F32),
        scratch_shapes=[
            pltpu.VMEM((tm, tf), F32),
            pltpu.VMEM((2, D, tf), BF16), pltpu.VMEM((2, D, tf), BF16), pltpu.VMEM((2, tf, D), BF16),
            pltpu.SemaphoreType.DMA(()), pltpu.SemaphoreType.DMA((3, 2)),
        ],
        compiler_params=_params(("arbitrary",), FFN_VMEM_LIMIT_BYTES),
        name="swiglu_ffn",
    )(hb, ssq, h, wg_bf, wu_bf, wd_bf)


def _rope_tables(seq):
    half = HEAD_DIM // 2
    freqs = ROPE_THETA ** (-jnp.arange(half, dtype=F32) * 2.0 / HEAD_DIM)
    ang = jnp.arange(seq, dtype=F32)[:, None] * freqs[None, :]
    cos = jnp.cos(ang)
    sin = jnp.sin(ang)
    return jnp.concatenate([cos, cos], axis=-1), jnp.concatenate([-sin, sin], axis=-1)


def _layer(x, p):
    B, S, D = x.shape
    T = B * S
    attn_w = D // 2
    conv_w = D - attn_w
    x2 = x.reshape(T, D)
    qkv, u = _in_projection(x2, p["norm_mix_g"], p["w_in"], p["cos"], p["sin"], p["q_norm_g"], p["k_norm_g"],
                            seq=S, attn_w=attn_w, conv_w=conv_w)
    attn = _attention(qkv.reshape(B, S, -1), attn_w=attn_w)
    conv = _conformer_conv(u, p["conv_w"], p["conv_b"], p["conv_ln_g"], p["conv_ln_b"],
                           seq=S, conv_width=conv_w)
    h, hb, ssq = _out_projection(attn.reshape(T, attn_w), conv, p["w_out"], x2, p["norm_ffn_g"])
    y = _ffn(hb, ssq, h, p["w_gate"], p["w_up"], p["w_down"])
    return y.reshape(B, S, D)


def kernel(x_prompt, x_sample, norm_mix_g, w_in, q_norm_g, k_norm_g, conv_w, conv_b, conv_ln_g, conv_ln_b,
           w_out, norm_ffn_g, w_gate, w_up, w_down):
    depth = w_in.shape[0]
    y_prompt, y_sample = x_prompt, x_sample
    cos, sin = _rope_tables(x_prompt.shape[1])
    assert x_sample.shape[1] == x_prompt.shape[1]
    row = lambda v: v.reshape(1, -1)
    for l in range(depth):
        p = dict(
            norm_mix_g=row(norm_mix_g[l]), w_in=w_in[l].astype(BF16), cos=cos, sin=sin,
            q_norm_g=row(q_norm_g[l]), k_norm_g=row(k_norm_g[l]),
            conv_w=conv_w[l], conv_b=row(conv_b[l]), conv_ln_g=row(conv_ln_g[l]), conv_ln_b=row(conv_ln_b[l]),
            w_out=w_out[l].astype(BF16), norm_ffn_g=row(norm_ffn_g[l]),
            w_gate=w_gate[l].astype(BF16), w_up=w_up[l].astype(BF16), w_down=w_down[l].astype(BF16),
        )
        y_prompt = _layer(y_prompt, p)
        y_sample = _layer(y_sample, p)
    return (y_prompt, y_sample)
```

```python
import functools

import numpy as np
import jax
import jax.numpy as jnp
from jax import lax
from jax.experimental import pallas as pl
from jax.experimental.pallas import tpu as pltpu

F32 = jnp.float32
BF16 = jnp.bfloat16

HEAD_DIM = 128
LANES = 128
CONV_TAPS = 31
CONV_PAD = (CONV_TAPS - 1) // 2
CONV_HALO = 16
DILATIONS = (1, 4, 16)
BAND = 64
QBLK = 128
KWIN = QBLK + 2 * BAND
GROUP = 16
ROPE_THETA = 10000.0
EPS = 1e-6
NEG = -1e30
VMEM_LIMIT_BYTES = 56 * 1024 * 1024
FFN_VMEM_LIMIT_BYTES = 60 * 1024 * 1024
W_TILE = 512


def _params(semantics, vmem_limit_bytes=VMEM_LIMIT_BYTES):
    return pltpu.CompilerParams(dimension_semantics=semantics, vmem_limit_bytes=vmem_limit_bytes)


def _inproj_kernel(x_ref, g_ref, wa_ref, wb_ref, cos_ref, sin_ref, qg_ref, kg_ref, qkv_ref, u_ref, hn0_ref, hn1_ref,
                   *, n_q, n_qkv, scale, norm_rows, n_chunks):
    i = pl.program_id(0)
    j = pl.program_id(1)
    tm, tn = u_ref.shape
    chunk_rows = tm // n_chunks

    def normalise(hn_ref, first_row, n_rows):
        for t in range(n_rows // norm_rows):
            rows = pl.ds(pl.multiple_of(first_row + t * norm_rows, norm_rows), norm_rows)
            x = x_ref[rows, :]
            ms = jnp.mean(x * x, axis=-1, keepdims=True)
            hn_ref[rows, :] = (x * lax.rsqrt(ms + EPS) * g_ref[...]).astype(BF16)

    @pl.when((i == 0) & (j == 0))
    def _():
        def body(t, c):
            normalise(hn0_ref, t * chunk_rows, chunk_rows)
            return c
        lax.fori_loop(0, n_chunks, body, 0)

    def steps(hn_ref, hn_next_ref):
        def prepare_next_tile():
            chunk = jnp.clip(j - 1, 0, n_chunks - 1)
            normalise(hn_next_ref, chunk * chunk_rows, chunk_rows)

        def proj(t):
            w_ref = (wa_ref, wb_ref)[t]
            return jnp.dot(hn_ref[...], w_ref[...], preferred_element_type=F32)

        @pl.when(j < 2 * n_q)
        def _():
            prepare_next_tile()
            gain = jnp.where(j < n_q, qg_ref[...] * scale, kg_ref[...])
            cos = cos_ref[...]
            sin = sin_ref[...]
            for t in range(2):
                acc = proj(t)
                heads = [acc[:, h * HEAD_DIM:(h + 1) * HEAD_DIM] for h in range(tn // HEAD_DIM)]
                ms = [jnp.mean(xh * xh, axis=-1, keepdims=True) for xh in heads]
                ys = [xh * lax.rsqrt(m + EPS) * gain for xh, m in zip(heads, ms)]
                rot = [pltpu.roll(y, HEAD_DIM // 2, 1) for y in ys]
                for h, (y, yr) in enumerate(zip(ys, rot)):
                    col = t * tn + h * HEAD_DIM
                    qkv_ref[:, col:col + HEAD_DIM] = (y * cos + yr * sin).astype(BF16)

        @pl.when((j >= 2 * n_q) & (j < n_qkv))
        def _():
            prepare_next_tile()
            for t in range(2):
                qkv_ref[:, t * tn:(t + 1) * tn] = proj(t).astype(BF16)

        @pl.when(j >= n_qkv)
        def _():
            prepare_next_tile()
            u_ref[...] = (proj(0) * jax.nn.sigmoid(proj(1))).astype(BF16)

    @pl.when(lax.rem(i, 2) == 0)
    def _():
        steps(hn0_ref, hn1_ref)

    @pl.when(lax.rem(i, 2) == 1)
    def _():
        steps(hn1_ref, hn0_ref)


def _in_projection(x2, norm_g, w_in_bf, cos_t, sin_t, qg, kg, *, seq, attn_w, conv_w, tm=512, tn=W_TILE):
    T, D = x2.shape
    n_q = attn_w // (2 * tn)
    n_qkv = 3 * n_q
    n_c = conv_w // tn
    n_steps = n_qkv + n_c
    n_i = T // tm
    kern = functools.partial(_inproj_kernel, n_q=n_q, n_qkv=n_qkv, scale=HEAD_DIM ** -0.5, norm_rows=16,
                             n_chunks=8)
    return pl.pallas_call(
        kern,
        grid=(n_i, n_steps),
        in_specs=[
            pl.BlockSpec((tm, D), lambda i, j: (jnp.minimum(i + jnp.minimum(j, 1), n_i - 1), 0)),
            pl.BlockSpec((1, D), lambda i, j: (0, 0)),
            pl.BlockSpec((D, tn), lambda i, j: (0, jnp.where(j < n_qkv, 2 * j, j + n_qkv))),
            pl.BlockSpec((D, tn), lambda i, j: (0, jnp.where(j < n_qkv, 2 * j + 1, j + n_qkv + n_c))),
            pl.BlockSpec((tm, HEAD_DIM), lambda i, j: (i % (seq // tm), 0)),
            pl.BlockSpec((tm, HEAD_DIM), lambda i, j: (i % (seq // tm), 0)),
            pl.BlockSpec((1, HEAD_DIM), lambda i, j: (0, 0)),
            pl.BlockSpec((1, HEAD_DIM), lambda i, j: (0, 0)),
        ],
        out_specs=[
            pl.BlockSpec((tm, 2 * tn), lambda i, j: (i, jnp.minimum(j, n_qkv - 1))),
            pl.BlockSpec((tm, tn), lambda i, j: (i, jnp.maximum(j - n_qkv, 0))),
        ],
        out_shape=[
            jax.ShapeDtypeStruct((T, 3 * attn_w), BF16),
            jax.ShapeDtypeStruct((T, conv_w), BF16),
        ],
        scratch_shapes=[pltpu.VMEM((tm, D), BF16), pltpu.VMEM((tm, D), BF16)],
        compiler_params=_params(("arbitrary", "arbitrary")),
        name="in_projection",
    )(x2, norm_g, w_in_bf, w_in_bf, cos_t, sin_t, qg, kg)


def _softmax_blocks(q, k, v, bias):
    s = jnp.einsum("gqd,gkd->gqk", q, k, preferred_element_type=F32) + bias
    m = jnp.max(s, axis=-1, keepdims=True)
    p = jnp.exp(s - m)
    l = jnp.sum(p, axis=-1, keepdims=True)
    o = jnp.einsum("gqk,gkd->gqd", p.astype(BF16), v, preferred_element_type=F32) * (1.0 / l)
    return o, m + jnp.log(l)


def _attn_kernel(q_ref, k_ref, v_ref, bias1_ref, bias4_ref, bias16_ref, o_ref,
                 f32buf, f32buf4, qp4, kp4, vp4, qp16, kp16, vp16, ob1, ob4, ob16, ls1, ls4, ls16,
                 *, seq, heads):
    l4 = seq // 4
    l16 = seq // 16
    nb4 = l4 // QBLK
    ng1 = seq // (GROUP * QBLK)
    grows = GROUP * QBLK
    blocks = lambda x: x.reshape(GROUP, QBLK, HEAD_DIM)

    def put(ob, ls, rows, o, lse):
        ob[rows, :] = o
        ls[rows, :] = jnp.broadcast_to(lse, o.shape)

    for h in range(heads):
        cols = slice(h * HEAD_DIM, (h + 1) * HEAD_DIM)

        for src, p4, p16 in ((q_ref, qp4, qp16), (k_ref, kp4, kp16), (v_ref, vp4, vp16)):
            f32buf[...] = src[0, :, cols].astype(F32)
            for r4 in range(4):
                x4 = f32buf[pl.ds(r4, l4, stride=4), :]
                f32buf4[r4 * l4:(r4 + 1) * l4, :] = x4
                p4[r4 * l4:(r4 + 1) * l4, :] = x4.astype(BF16)
            for r in range(16):
                a, r4 = divmod(r, 4)
                p16[r * l16:(r + 1) * l16, :] = f32buf4[pl.ds(r4 * l4 + a, l16, stride=4), :].astype(BF16)


        def body1(g, c):
            rows = pl.ds(pl.multiple_of(g * grows, grows), grows)
            starts = [pl.multiple_of(jnp.clip((g * GROUP + j) * QBLK - BAND, 0, seq - KWIN), BAND)
                      for j in range(GROUP)]
            k3 = jnp.stack([k_ref[0, pl.ds(ks, KWIN), cols] for ks in starts])
            v3 = jnp.stack([v_ref[0, pl.ds(ks, KWIN), cols] for ks in starts])
            o, lse = _softmax_blocks(blocks(q_ref[0, rows, cols]), k3, v3, bias1_ref[g])
            put(ob1, ls1, rows, o.reshape(grows, HEAD_DIM), lse.reshape(grows, 1))
            return c
        lax.fori_loop(0, ng1, body1, 0)

        def body4(g, c):
            base = pl.multiple_of(g * grows, grows)
            rows = pl.ds(base, grows)
            starts = [(j // nb4) * l4 + min(max((j % nb4) * QBLK - BAND, 0), l4 - KWIN) for j in range(GROUP)]
            k3 = jnp.stack([kp4[pl.ds(base + ks, KWIN), :] for ks in starts])
            v3 = jnp.stack([vp4[pl.ds(base + ks, KWIN), :] for ks in starts])
            o, lse = _softmax_blocks(blocks(qp4[rows, :]), k3, v3, bias4_ref[...])
            put(ob4, ls4, rows, o.reshape(grows, HEAD_DIM), lse.reshape(grows, 1))
            return c
        lax.fori_loop(0, seq // grows, body4, 0)

        def body16(g, c):
            rows = pl.ds(pl.multiple_of(g * grows, grows), grows)
            o, lse = _softmax_blocks(blocks(qp16[rows, :]), blocks(kp16[rows, :]), blocks(vp16[rows, :]),
                                     bias16_ref[...])
            for j in range(GROUP):
                a = g * (GROUP // 4) + j // 4
                put(ob16, ls16, pl.ds((j % 4) * l4 + a, l16, stride=4), o[j], lse[j])
            return c
        lax.fori_loop(0, 16 // GROUP, body16, 0)

        rc = 64
        for r4 in range(4):
            def comb(t, c, r4=r4):
                rows = pl.ds(pl.multiple_of(r4 * l4 + t * rc, rc), rc)
                nat = pl.ds(4 * t * rc + r4, rc, stride=4)
                a1, a4, a16 = ls1[nat, :], ls4[rows, :], ls16[rows, :]
                mx = jnp.maximum(jnp.maximum(a1, a4), a16)
                e1, e4, e16 = jnp.exp(a1 - mx), jnp.exp(a4 - mx), jnp.exp(a16 - mx)
                out = (e1 * ob1[nat, :] + e4 * ob4[rows, :] + e16 * ob16[rows, :]) / (e1 + e4 + e16)
                f32buf[nat, :] = out
                return c
            lax.fori_loop(0, l4 // rc, comb, 0, unroll=2)
        o_ref[0, :, cols] = f32buf[...].astype(BF16)


def _band_biases(seq):
    r = np.arange(QBLK)[:, None]
    c = np.arange(KWIN)[None, :]
    first = np.abs(c - r) <= BAND
    mid = np.abs(c - BAND - r) <= BAND
    last = np.abs(c - (KWIN - QBLK) - r) <= BAND
    neg = lambda valid: np.where(valid, 0.0, NEG).astype(np.float32)
    nb1 = seq // QBLK
    b1 = neg(np.stack([first] + [mid] * (nb1 - 2) + [last])).reshape(nb1 // GROUP, GROUP, QBLK, KWIN)
    nb4 = seq // (4 * QBLK)
    b4 = neg(np.stack(([first] + [mid] * (nb4 - 2) + [last]) * (GROUP // nb4)))
    c16 = np.arange(seq // 16)[None, :]
    r16 = np.arange(seq // 16)[:, None]
    b16 = neg(np.abs(c16 - r16) <= BAND)
    return jnp.asarray(b1), jnp.asarray(b4), jnp.asarray(b16)


def _attention(proj3, *, attn_w, heads=2):
    B, S, _ = proj3.shape
    assert S // 16 == QBLK and GROUP % (S // (4 * QBLK)) == 0 and S % (GROUP * QBLK) == 0 and GROUP % 4 == 0
    n_hg = attn_w // (heads * HEAD_DIM)
    hw = heads * HEAD_DIM
    b1, b4, b16 = _band_biases(S)
    kern = functools.partial(_attn_kernel, seq=S, heads=heads)
    perm = lambda: pltpu.VMEM((S, HEAD_DIM), BF16)
    nat = lambda: pltpu.VMEM((S, HEAD_DIM), F32)
    return pl.pallas_call(
        kern,
        grid=(B, n_hg),
        in_specs=[
            pl.BlockSpec((1, S, hw), lambda b, g: (b, 0, g)),
            pl.BlockSpec((1, S, hw), lambda b, g: (b, 0, n_hg + g)),
            pl.BlockSpec((1, S, hw), lambda b, g: (b, 0, 2 * n_hg + g)),
            pl.BlockSpec((S // (GROUP * QBLK), GROUP, QBLK, KWIN), lambda b, g: (0, 0, 0, 0)),
            pl.BlockSpec((GROUP, QBLK, KWIN), lambda b, g: (0, 0, 0)),
            pl.BlockSpec((S // 16, S // 16), lambda b, g: (0, 0)),
        ],
        out_specs=pl.BlockSpec((1, S, hw), lambda b, g: (b, 0, g)),
        out_shape=jax.ShapeDtypeStruct((B, S, attn_w), BF16),
        scratch_shapes=[nat(), nat()] + [perm() for _ in range(6)] + [nat() for _ in range(6)],
        compiler_params=_params(("parallel", "parallel")),
        name="dilated_attention",
    )(proj3, proj3, proj3, b1, b4, b16)


def _conv_kernel(prev_ref, main_ref, next_ref, w_ref, b_ref, lg_ref, lb_ref, o_ref, ubuf, ybuf,
                 *, tiles_per_seq, row_blk, norm_rows):
    tt, C = main_ref.shape
    n_cb, _, col_blk = ybuf.shape
    pos = pl.program_id(0) % tiles_per_seq
    zero_halo = jnp.zeros((CONV_HALO, col_blk), F32)
    for cb in range(n_cb):
        cols = slice(cb * col_blk, (cb + 1) * col_blk)
        ubuf[cb, 0:CONV_HALO, :] = jnp.where(pos != 0, prev_ref[:, cols].astype(F32), zero_halo)
        ubuf[cb, CONV_HALO:CONV_HALO + tt, :] = main_ref[:, cols].astype(F32)
        ubuf[cb, CONV_HALO + tt:, :] = jnp.where(pos != tiles_per_seq - 1, next_ref[:, cols].astype(F32),
                                                 zero_halo)

    def conv_body(cb, c):
        for rb in range(tt // (2 * row_blk)):
            for parity in range(2):
                first = rb * 2 * row_blk + parity
                acc = jnp.zeros((row_blk, col_blk), F32)
                for k in range(CONV_TAPS):
                    r = first + CONV_HALO - CONV_PAD + k
                    acc = acc + ubuf[cb, pl.ds(r, row_blk, stride=2), :] * w_ref[cb, k:k + 1, :]
                ybuf[cb, pl.ds(first, row_blk, stride=2), :] = acc + b_ref[cb]
        return c
    lax.fori_loop(0, n_cb, conv_body, 0)

    def norm_body(rb, c):
        rows = pl.ds(pl.multiple_of(rb * norm_rows, norm_rows), norm_rows)
        ys = [ybuf[cb, rows, :] for cb in range(n_cb)]
        mu = jnp.sum(sum(ys), axis=-1, keepdims=True) * (1.0 / C)
        ycs = [y - mu for y in ys]
        var = jnp.sum(sum(yc * yc for yc in ycs), axis=-1, keepdims=True) * (1.0 / C)
        rstd = lax.rsqrt(var + EPS)
        for cb in range(n_cb):
            cols = slice(cb * col_blk, (cb + 1) * col_blk)
            z = ycs[cb] * rstd * lg_ref[:, cols] + lb_ref[:, cols]
            o_ref[rows, cols] = (z * jax.nn.sigmoid(z)).astype(BF16)
        return c
    lax.fori_loop(0, tt // norm_rows, norm_body, 0, unroll=2)


def _conformer_conv(proj2, conv_w, conv_b, ln_g, ln_b, *, seq, conv_width, tt=512, col_blk=LANES):
    T, W = proj2.shape
    C = conv_width
    n_cb = C // col_blk
    ucol = (W - C) // C
    hb = tt // CONV_HALO
    n_halo = T // CONV_HALO
    w_blk = conv_w.reshape(CONV_TAPS, n_cb, col_blk).transpose(1, 0, 2)
    b_blk = conv_b.reshape(n_cb, 1, col_blk)
    kern = functools.partial(_conv_kernel, tiles_per_seq=seq // tt, row_blk=tt // 2, norm_rows=64)
    vec = lambda: pl.BlockSpec((1, C), lambda i: (0, 0))
    return pl.pallas_call(
        kern,
        grid=(T // tt,),
        in_specs=[
            pl.BlockSpec((CONV_HALO, C), lambda i: (jnp.maximum(i * hb - 1, 0), ucol)),
            pl.BlockSpec((tt, C), lambda i: (i, ucol)),
            pl.BlockSpec((CONV_HALO, C), lambda i: (jnp.minimum((i + 1) * hb, n_halo - 1), ucol)),
            pl.BlockSpec((n_cb, CONV_TAPS, col_blk), lambda i: (0, 0, 0)),
            pl.BlockSpec((n_cb, 1, col_blk), lambda i: (0, 0, 0)),
            vec(), vec(),
        ],
        out_specs=pl.BlockSpec((tt, C), lambda i: (i, 0)),
        out_shape=jax.ShapeDtypeStruct((T, C), BF16),
        scratch_shapes=[pltpu.VMEM((n_cb, tt + 2 * CONV_HALO, col_blk), F32),
                        pltpu.VMEM((n_cb, tt, col_blk), F32)],
        compiler_params=_params(("parallel",)),
        name="conformer_conv",
    )(proj2, proj2, proj2, w_blk, b_blk, ln_g, ln_b)


def _outproj_kernel(attn_ref, conv_ref, w_ref, x_ref, g_ref, h_ref, hb_ref, ssq_ref):
    j = pl.program_id(1)
    A = attn_ref.shape[1]
    acc = jnp.dot(attn_ref[...], w_ref[0:A, :], preferred_element_type=F32)
    acc = acc + jnp.dot(conv_ref[...], w_ref[A:, :], preferred_element_type=F32)
    h = x_ref[...] + acc
    h_ref[...] = h
    hb_ref[...] = (h * g_ref[...]).astype(BF16)
    hh = h * h
    part = hh[:, 0:LANES]
    for c in range(1, h.shape[1] // LANES):
        part = part + hh[:, c * LANES:(c + 1) * LANES]

    @pl.when(j == 0)
    def _():
        ssq_ref[...] = part

    @pl.when(j != 0)
    def _():
        ssq_ref[...] += part


def _out_projection(attn2, conv2, w_out_bf, x2, ffn_g, *, tm=1024, tn=W_TILE):
    T, D = x2.shape
    A = attn2.shape[1]
    C = conv2.shape[1]
    return pl.pallas_call(
        _outproj_kernel,
        grid=(T // tm, D // tn),
        in_specs=[
            pl.BlockSpec((tm, A), lambda i, j: (i, 0)),
            pl.BlockSpec((tm, C), lambda i, j: (i, 0)),
            pl.BlockSpec((A + C, tn), lambda i, j: (0, j)),
            pl.BlockSpec((tm, tn), lambda i, j: (i, j)),
            pl.BlockSpec((1, tn), lambda i, j: (0, j)),
        ],
        out_specs=[
            pl.BlockSpec((tm, tn), lambda i, j: (i, j)),
            pl.BlockSpec((tm, tn), lambda i, j: (i, j)),
            pl.BlockSpec((tm, LANES), lambda i, j: (i, 0)),
        ],
        out_shape=[
            jax.ShapeDtypeStruct((T, D), F32),
            jax.ShapeDtypeStruct((T, D), BF16),
            jax.ShapeDtypeStruct((T, LANES), F32),
        ],
        compiler_params=_params(("parallel", "arbitrary")),
        name="out_projection",
    )(attn2, conv2, w_out_bf, x2, ffn_g)


def _ffn_kernel(hb_ref, ssq_ref, h_hbm, wg_hbm, wu_hbm, wd_hbm, o_ref, r_ref, wg_buf, wu_buf, wd_buf,
                sem_res, sem_w, *, d_model, d_ff):
    i = pl.program_id(0)
    ni = pl.num_programs(0)
    tm = o_ref.shape[0]
    tf = wg_buf.shape[2]
    n_full = d_ff // tf
    tail = d_ff - n_full * tf

    def residual_copy():
        return pltpu.make_async_copy(h_hbm.at[pl.ds(pl.multiple_of(i * tm, tm), tm), :], o_ref, sem_res)

    def tile_copies(t, slot, width):
        start = t * tf if isinstance(t, int) else pl.multiple_of(t * tf, tf)
        src = pl.ds(start, width)
        dst = pl.ds(0, width)
        return (
            pltpu.make_async_copy(wg_hbm.at[:, src], wg_buf.at[slot, :, dst], sem_w.at[0, slot]),
            pltpu.make_async_copy(wu_hbm.at[:, src], wu_buf.at[slot, :, dst], sem_w.at[1, slot]),
            pltpu.make_async_copy(wd_hbm.at[src, :], wd_buf.at[slot, dst, :], sem_w.at[2, slot]),
        )

    def start_tile(t, slot, width):
        for c in tile_copies(t, slot, width):
            c.start()

    def wait_tile(t, slot, width):
        for c in tile_copies(t, slot, width):
            c.wait()

    def multiply(slot, width, before_accumulate=None):
        hb = hb_ref[...]
        r = r_ref[:, :width]
        g = jnp.dot(hb, wg_buf[slot, :, :width], preferred_element_type=F32) * r
        u = jnp.dot(hb, wu_buf[slot, :, :width], preferred_element_type=F32) * r
        a = (g * jax.nn.sigmoid(g) * u).astype(BF16)
        if before_accumulate is not None:
            before_accumulate()
        o_ref[...] += jnp.dot(a, wd_buf[slot, :width, :], preferred_element_type=F32)

    residual_copy().start()

    @pl.when(i == 0)
    def _():
        start_tile(0, 0, tf)

    ms = jnp.sum(ssq_ref[...], axis=-1, keepdims=True) * (1.0 / d_model)
    r_ref[...] = jnp.broadcast_to(lax.rsqrt(ms + EPS), r_ref.shape)

    def pair(p, c):
        t = 2 * p
        wait_tile(t, 0, tf)
        start_tile(t + 1, 1, tf)

        def residual_landed():
            @pl.when(p == 0)
            def _():
                residual_copy().wait()
        multiply(0, tf, residual_landed)
        wait_tile(t + 1, 1, tf)
        start_tile(t + 2, 0, tf)
        multiply(1, tf)
        return c
    lax.fori_loop(0, (n_full - 1) // 2, pair, 0)

    wait_tile(n_full - 1, 0, tf)
    start_tile(n_full, 1, tail)
    multiply(0, tf)
    wait_tile(n_full, 1, tail)

    @pl.when(i + 1 < ni)
    def _():
        start_tile(0, 0, tf)
    multiply(1, tail)


def _ffn(hb, ssq, h, wg_bf, wu_bf, wd_bf, *, tm=512, tf=W_TILE):
    T, D = h.shape
    d_ff = wg_bf.shape[1]
    assert (d_ff // tf) % 2 == 1 and d_ff // tf >= 3 and d_ff % tf != 0 and d_ff % LANES == 0
    kern = functools.partial(_ffn_kernel, d_model=D, d_ff=d_ff)
    hbm = pl.BlockSpec(memory_space=pl.ANY)
    return pl.pallas_call(
        kern,
        grid=(T // tm,),
        in_specs=[
            pl.BlockSpec((tm, D), lambda i: (i, 0)),
            pl.BlockSpec((tm, LANES), lambda i: (i, 0)),
            hbm, hbm, hbm, hbm,
        ],
        out_specs=pl.BlockSpec((tm, D), lambda i: (i, 0)),
        out_shape=jax.ShapeDtypeStruct((T, D), F32),
        scratch_shapes=[
            pltpu.VMEM((tm, tf), F32),
            pltpu.VMEM((2, D, tf), BF16), pltpu.VMEM((2, D, tf), BF16), pltpu.VMEM((2, tf, D), BF16),
            pltpu.SemaphoreType.DMA(()), pltpu.SemaphoreType.DMA((3, 2)),
        ],
        compiler_params=_params(("arbitrary",), FFN_VMEM_LIMIT_BYTES),
        name="swiglu_ffn",
    )(hb, ssq, h, wg_bf, wu_bf, wd_bf)


def _rope_tables(seq):
    half = HEAD_DIM // 2
    freqs = ROPE_THETA ** (-jnp.arange(half, dtype=F32) * 2.0 / HEAD_DIM)
    ang = jnp.arange(seq, dtype=F32)[:, None] * freqs[None, :]
    cos = jnp.cos(ang)
    sin = jnp.sin(ang)
    return jnp.concatenate([cos, cos], axis=-1), jnp.concatenate([-sin, sin], axis=-1)


def _layer(x, p):
    B, S, D = x.shape
    T = B * S
    attn_w = D // 2
    conv_w = D - attn_w
    x2 = x.reshape(T, D)
    qkv, u = _in_projection(x2, p["norm_mix_g"], p["w_in"], p["cos"], p["sin"], p["q_norm_g"], p["k_norm_g"],
                            seq=S, attn_w=attn_w, conv_w=conv_w)
    attn = _attention(qkv.reshape(B, S, -1), attn_w=attn_w)
    conv = _conformer_conv(u, p["conv_w"], p["conv_b"], p["conv_ln_g"], p["conv_ln_b"],
                           seq=S, conv_width=conv_w)
    h, hb, ssq = _out_projection(attn.reshape(T, attn_w), conv, p["w_out"], x2, p["norm_ffn_g"])
    y = _ffn(hb, ssq, h, p["w_gate"], p["w_up"], p["w_down"])
    return y.reshape(B, S, D)


def kernel(x_prompt, x_sample, norm_mix_g, w_in, q_norm_g, k_norm_g, conv_w, conv_b, conv_ln_g, conv_ln_b,
           w_out, norm_ffn_g, w_gate, w_up, w_down):
    depth = w_in.shape[0]
    y_prompt, y_sample = x_prompt, x_sample
    cos, sin = _rope_tables(x_prompt.shape[1])
    assert x_sample.shape[1] == x_prompt.shape[1]
    row = lambda v: v.reshape(1, -1)
    for l in range(depth):
        p = dict(
            norm_mix_g=row(norm_mix_g[l]), w_in=w_in[l].astype(BF16), cos=cos, sin=sin,
            q_norm_g=row(q_norm_g[l]), k_norm_g=row(k_norm_g[l]),
            conv_w=conv_w[l], conv_b=row(conv_b[l]), conv_ln_g=row(conv_ln_g[l]), conv_ln_b=row(conv_ln_b[l]),
            w_out=w_out[l].astype(BF16), norm_ffn_g=row(norm_ffn_g[l]),
            w_gate=w_gate[l].astype(BF16), w_up=w_up[l].astype(BF16), w_down=w_down[l].astype(BF16),
        )
        y_prompt = _layer(y_prompt, p)
        y_sample = _layer(y_sample, p)
    return (y_prompt, y_sample)
```
